```python
import math, functools
import jax, jax.numpy as jnp
from jax import lax
import numpy as np

D_MODEL = 4096
BATCH = 2
SEQ = 4096
DEPTH = 1
DEC_BATCH = 128
DEC_SEQ = 8
PAST_LEN = 2048
PAGE_SIZE = 128

HEAD_DIM = 128
D_ATT = D_MODEL // 2
D_SSM = D_MODEL - D_ATT
D_MIX = D_ATT + D_SSM
N_ATT_HEADS = D_ATT // HEAD_DIM
DILATED_GROUPS = ((128, 1), (512, 4), (2048, 16))
MAX_WINDOW = max(w for w, _ in DILATED_GROUPS)
Q_BLOCK = 128
SSM_GROUP_CH = 16
N_SSM_GROUPS = D_SSM // SSM_GROUP_CH
SSM_STATE = 64
DT_MIN = 0.001
DT_MAX = 0.1
D_FF = ((8 * D_MODEL // 3 + 255) // 256) * 256
CONV_W = 3
NORM_EPS = 1e-6

kernel_name = 'hymba_dilated_attn_s5_convffn_step'


def rms_norm(x, g):
    xf = x.astype(jnp.float32)
    y = xf * lax.rsqrt(jnp.mean(xf * xf, axis=-1, keepdims=True) + NORM_EPS)
    return (y * g.astype(jnp.float32)).astype(x.dtype)


def alibi_slopes(n_heads):
    return 2.0 ** (-8.0 * jnp.arange(1, n_heads + 1, dtype=jnp.float32) / n_heads)


def dilated_window_attention(q, k_ctx, v_ctx, q_pos, ctx_start):
    L = k_ctx.shape[1]
    slopes = alibi_slopes(q.shape[2])
    scale = HEAD_DIM ** -0.5
    outs, lses = [], []
    for window, dilation in DILATED_GROUPS:
        dist = jnp.arange(window // dilation + 1, dtype=jnp.int32) * dilation
        k_pos = q_pos[:, None] - dist[None, :]
        valid = k_pos >= 0
        idx = jnp.clip(k_pos - ctx_start, 0, L - 1)
        k_g = k_ctx[:, idx]
        v_g = v_ctx[:, idx]
        s = jnp.einsum('bqhd,bqjhd->bhqj', q, k_g, preferred_element_type=jnp.float32) * scale
        s = s - slopes[:, None, None] * dist.astype(jnp.float32)
        s = jnp.where(valid[None, None], s, -jnp.inf)
        lse = jax.nn.logsumexp(s, axis=-1)
        p = jnp.exp(s - lse[..., None])
        outs.append(jnp.einsum('bhqj,bqjhd->bqhd', p.astype(v_g.dtype), v_g))
        lses.append(lse)
    w = jax.nn.softmax(jnp.stack(lses), axis=0)
    w = jnp.transpose(w, (0, 1, 3, 2))[..., None]
    out = jnp.sum(w * jnp.stack(outs).astype(jnp.float32), axis=0)
    return out.astype(q.dtype)


def attention_prompt(q, k, v):
    B, S = q.shape[0], q.shape[1]
    pad = ((0, 0), (MAX_WINDOW, 0), (0, 0), (0, 0))
    k_pad = jnp.pad(k, pad)
    v_pad = jnp.pad(v, pad)
    ctx_len = MAX_WINDOW + Q_BLOCK

    def block(b0):
        q_b = lax.dynamic_slice_in_dim(q, b0, Q_BLOCK, axis=1)
        k_b = lax.dynamic_slice_in_dim(k_pad, b0, ctx_len, axis=1)
        v_b = lax.dynamic_slice_in_dim(v_pad, b0, ctx_len, axis=1)
        q_pos = b0 + jnp.arange(Q_BLOCK, dtype=jnp.int32)
        return dilated_window_attention(q_b, k_b, v_b, q_pos, b0 - MAX_WINDOW)

    starts = jnp.arange(S // Q_BLOCK, dtype=jnp.int32) * Q_BLOCK
    o = lax.map(block, starts)
    return jnp.moveaxis(o, 0, 1).reshape(q.shape)


def attention_sample(q, k, v, cache_k, cache_v):
    win_buf = cache_k.shape[1]
    k_ctx = jnp.concatenate([cache_k.astype(k.dtype), k], axis=1)
    v_ctx = jnp.concatenate([cache_v.astype(v.dtype), v], axis=1)
    q_pos = PAST_LEN + jnp.arange(q.shape[1], dtype=jnp.int32)
    return dilated_window_attention(q, k_ctx, v_ctx, q_pos, PAST_LEN - win_buf)


def cmul(ar, ai, br, bi):
    return ar * br - ai * bi, ar * bi + ai * br


def s5_discretise(a_re, a_im, log_dt, b_re, b_im):
    dt = jnp.exp(log_dt.astype(jnp.float32))[:, None]
    ar, ai = a_re.astype(jnp.float32), a_im.astype(jnp.float32)
    mag = jnp.exp(ar * dt)
    lb_re, lb_im = mag * jnp.cos(ai * dt), mag * jnp.sin(ai * dt)
    den = ar * ar + ai * ai
    c_re, c_im = cmul(lb_re - 1.0, lb_im, ar / den, -ai / den)
    bb_re, bb_im = cmul(c_re[..., None], c_im[..., None],
                        b_re.astype(jnp.float32), b_im.astype(jnp.float32))
    return lb_re, lb_im, bb_re, bb_im


def s5_mixer(u, x0_re, x0_im, a_re, a_im, log_dt, b_re, b_im, c_re, c_im, d_skip):
    Bsz, L, _ = u.shape
    uf = u.astype(jnp.float32).reshape(Bsz, L, N_SSM_GROUPS, SSM_GROUP_CH)
    lb_re, lb_im, bb_re, bb_im = s5_discretise(a_re, a_im, log_dt, b_re, b_im)
    bu_re = jnp.einsum('blgc,gpc->blgp', uf, bb_re)
    bu_im = jnp.einsum('blgc,gpc->blgp', uf, bb_im)
    i0_re, i0_im = cmul(lb_re, lb_im, x0_re.astype(jnp.float32), x0_im.astype(jnp.float32))
    bu_re = bu_re.at[:, 0].add(i0_re)
    bu_im = bu_im.at[:, 0].add(i0_im)
    a_re_t = jnp.broadcast_to(lb_re, bu_re.shape)
    a_im_t = jnp.broadcast_to(lb_im, bu_im.shape)

    def combine(e1, e2):
        a1r, a1i, b1r, b1i = e1
        a2r, a2i, b2r, b2i = e2
        ar, ai = cmul(a2r, a2i, a1r, a1i)
        br, bi = cmul(a2r, a2i, b1r, b1i)
        return ar, ai, br + b2r, bi + b2i

    _, _, xr, xi = lax.associative_scan(combine, (a_re_t, a_im_t, bu_re, bu_im), axis=1)
    y = (jnp.einsum('blgp,gcp->blgc', xr, c_re.astype(jnp.float32))
         - jnp.einsum('blgp,gcp->blgc', xi, c_im.astype(jnp.float32)))
    y = y.reshape(Bsz, L, D_SSM) + d_skip.astype(jnp.float32) * u.astype(jnp.float32)
    return y.astype(u.dtype), xr[:, -1].astype(x0_re.dtype), xi[:, -1].astype(x0_im.dtype)


def conv_ffn(h, conv_state, w_gate, w_up, conv_w, conv_b, w_down):
    L = h.shape[1]
    a = h @ w_gate
    a_full = jnp.concatenate([conv_state.astype(a.dtype), a], axis=1)
    conv = conv_b
    for i in range(CONV_W):
        conv = conv + conv_w[i] * a_full[:, i:i + L]
    out = (jax.nn.silu(conv) * (h @ w_up)) @ w_down
    return out, a_full[:, -(CONV_W - 1):]


def layer_forward(x, attend, ssm_re0, ssm_im0, conv_state0,
                  norm1_g, w_in, q_norm_g, k_norm_g,
                  ssm_a_re, ssm_a_im, ssm_log_dt, ssm_b_re, ssm_b_im, ssm_c_re, ssm_c_im, ssm_d,
                  w_glu, b_glu, w_out, norm2_g,
                  w_ffn_gate, w_ffn_up, ffn_conv_w, ffn_conv_b, w_ffn_down):
    B, L, _ = x.shape
    h = rms_norm(x, norm1_g)
    proj = h @ w_in
    q = proj[..., :D_ATT].reshape(B, L, N_ATT_HEADS, HEAD_DIM)
    k = proj[..., D_ATT:2 * D_ATT].reshape(B, L, N_ATT_HEADS, HEAD_DIM)
    v = proj[..., 2 * D_ATT:3 * D_ATT].reshape(B, L, N_ATT_HEADS, HEAD_DIM)
    u = proj[..., 3 * D_ATT:]
    q = rms_norm(q, q_norm_g)
    k = rms_norm(k, k_norm_g)
    o_att = attend(q, k, v).reshape(B, L, D_ATT)
    y_ssm, ssm_re, ssm_im = s5_mixer(u, ssm_re0, ssm_im0, ssm_a_re, ssm_a_im, ssm_log_dt,
                                     ssm_b_re, ssm_b_im, ssm_c_re, ssm_c_im, ssm_d)
    z = jax.nn.gelu(y_ssm)
    o_ssm = z * jax.nn.sigmoid(z @ w_glu + b_glu)
    x = x + jnp.concatenate([o_att, o_ssm], axis=-1) @ w_out
    h2 = rms_norm(x, norm2_g)
    f, conv_state = conv_ffn(h2, conv_state0, w_ffn_gate, w_ffn_up, ffn_conv_w, ffn_conv_b, w_ffn_down)
    x = x + f
    return x, k, v, ssm_re, ssm_im, conv_state


def setup_inputs(seed: int = 0) -> dict:
    key = jax.random.key(seed)
    ks = jax.random.split(key, 32)
    f32 = jnp.float32
    win_buf = min(MAX_WINDOW, PAST_LEN)
    n = lambda k, shape: jax.random.normal(k, shape, f32)
    x_prompt = n(ks[0], (BATCH, SEQ, D_MODEL))
    x_sample = n(ks[1], (DEC_BATCH, DEC_SEQ, D_MODEL))
    cache_k = n(ks[2], (DEPTH, DEC_BATCH, win_buf, N_ATT_HEADS, HEAD_DIM))
    cache_v = n(ks[3], (DEPTH, DEC_BATCH, win_buf, N_ATT_HEADS, HEAD_DIM))
    state_ssm_re = 0.1 * n(ks[4], (DEPTH, DEC_BATCH, N_SSM_GROUPS, SSM_STATE))
    state_ssm_im = 0.1 * n(ks[5], (DEPTH, DEC_BATCH, N_SSM_GROUPS, SSM_STATE))
    state_ffn_conv = n(ks[6], (DEPTH, DEC_BATCH, CONV_W - 1, D_FF))
    norm1_g = 1.0 + 0.02 * n(ks[7], (DEPTH, D_MODEL))
    w_in = n(ks[8], (DEPTH, D_MODEL, 3 * D_ATT + D_SSM)) * D_MODEL ** -0.5
    q_norm_g = 1.0 + 0.02 * n(ks[9], (DEPTH, HEAD_DIM))
    k_norm_g = 1.0 + 0.02 * n(ks[10], (DEPTH, HEAD_DIM))
    state_idx = jnp.arange(SSM_STATE, dtype=f32)
    ssm_a_re = -0.5 * jnp.exp(0.02 * n(ks[11], (DEPTH, N_SSM_GROUPS, SSM_STATE)))
    ssm_a_im = math.pi * state_idx + 0.01 * n(ks[12], (DEPTH, N_SSM_GROUPS, SSM_STATE))
    ssm_log_dt = jax.random.uniform(ks[13], (DEPTH, N_SSM_GROUPS), f32,
                                    math.log(DT_MIN), math.log(DT_MAX))
    ssm_b_re = n(ks[14], (DEPTH, N_SSM_GROUPS, SSM_STATE, SSM_GROUP_CH)) * (0.5 / SSM_GROUP_CH) ** 0.5
    ssm_b_im = n(ks[15], (DEPTH, N_SSM_GROUPS, SSM_STATE, SSM_GROUP_CH)) * (0.5 / SSM_GROUP_CH) ** 0.5
    ssm_c_re = n(ks[16], (DEPTH, N_SSM_GROUPS, SSM_GROUP_CH, SSM_STATE)) * (0.5 / SSM_STATE) ** 0.5
    ssm_c_im = n(ks[17], (DEPTH, N_SSM_GROUPS, SSM_GROUP_CH, SSM_STATE)) * (0.5 / SSM_STATE) ** 0.5
    ssm_d = n(ks[18], (DEPTH, D_SSM))
    w_glu = n(ks[19], (DEPTH, D_SSM, D_SSM)) * D_SSM ** -0.5
    b_glu = 0.01 * n(ks[20], (DEPTH, D_SSM))
    w_out = n(ks[21], (DEPTH, D_MIX, D_MODEL)) * D_MIX ** -0.5
    norm2_g = 1.0 + 0.02 * n(ks[22], (DEPTH, D_MODEL))
    w_ffn_gate = n(ks[23], (DEPTH, D_MODEL, D_FF)) * D_MODEL ** -0.5
    w_ffn_up = n(ks[24], (DEPTH, D_MODEL, D_FF)) * D_MODEL ** -0.5
    ffn_conv_w = n(ks[25], (DEPTH, CONV_W, D_FF)) * CONV_W ** -0.5
    ffn_conv_b = 0.01 * n(ks[26], (DEPTH, D_FF))
    w_ffn_down = n(ks[27], (DEPTH, D_FF, D_MODEL)) * D_FF ** -0.5
    return {'x_prompt': x_prompt, 'x_sample': x_sample,
            'cache_k': cache_k, 'cache_v': cache_v,
            'state_ssm_re': state_ssm_re, 'state_ssm_im': state_ssm_im, 'state_ffn_conv': state_ffn_conv,
            'norm1_g': norm1_g, 'w_in': w_in, 'q_norm_g': q_norm_g, 'k_norm_g': k_norm_g,
            'ssm_a_re': ssm_a_re, 'ssm_a_im': ssm_a_im, 'ssm_log_dt': ssm_log_dt,
            'ssm_b_re': ssm_b_re, 'ssm_b_im': ssm_b_im, 'ssm_c_re': ssm_c_re, 'ssm_c_im': ssm_c_im,
            'ssm_d': ssm_d, 'w_glu': w_glu, 'b_glu': b_glu, 'w_out': w_out, 'norm2_g': norm2_g,
            'w_ffn_gate': w_ffn_gate, 'w_ffn_up': w_ffn_up, 'ffn_conv_w': ffn_conv_w,
            'ffn_conv_b': ffn_conv_b, 'w_ffn_down': w_ffn_down}


def reference(x_prompt, x_sample, cache_k, cache_v, state_ssm_re, state_ssm_im, state_ffn_conv,
              norm1_g, w_in, q_norm_g, k_norm_g,
              ssm_a_re, ssm_a_im, ssm_log_dt, ssm_b_re, ssm_b_im, ssm_c_re, ssm_c_im,
              ssm_d, w_glu, b_glu, w_out, norm2_g,
              w_ffn_gate, w_ffn_up, ffn_conv_w, ffn_conv_b, w_ffn_down):
    n_prompt, seq = x_prompt.shape[0], x_prompt.shape[1]
    wb_prompt = min(MAX_WINDOW, seq)
    kp_l, vp_l, rep_l, imp_l, convp_l = [], [], [], [], []
    ks_l, vs_l, res_l, ims_l, convs_l = [], [], [], [], []
    yp, ys = x_prompt, x_sample
    for l in range(DEPTH):
        lw = (norm1_g[l], w_in[l], q_norm_g[l], k_norm_g[l],
              ssm_a_re[l], ssm_a_im[l], ssm_log_dt[l], ssm_b_re[l], ssm_b_im[l], ssm_c_re[l], ssm_c_im[l],
              ssm_d[l], w_glu[l], b_glu[l], w_out[l], norm2_g[l],
              w_ffn_gate[l], w_ffn_up[l], ffn_conv_w[l], ffn_conv_b[l], w_ffn_down[l])
        zeros_ssm = jnp.zeros((n_prompt, N_SSM_GROUPS, SSM_STATE), x_prompt.dtype)
        zeros_conv = jnp.zeros((n_prompt, CONV_W - 1, D_FF), x_prompt.dtype)
        yp, k_p, v_p, re_p, im_p, conv_p = layer_forward(
            yp, attention_prompt, zeros_ssm, zeros_ssm, zeros_conv, *lw)
        attend_s = functools.partial(attention_sample, cache_k=cache_k[l], cache_v=cache_v[l])
        ys, k_s, v_s, re_s, im_s, conv_s = layer_forward(
            ys, attend_s, state_ssm_re[l], state_ssm_im[l], state_ffn_conv[l], *lw)
        kp_l.append(k_p[:, seq - wb_prompt:])
        vp_l.append(v_p[:, seq - wb_prompt:])
        rep_l.append(re_p)
        imp_l.append(im_p)
        convp_l.append(conv_p)
        ks_l.append(k_s)
        vs_l.append(v_s)
        res_l.append(re_s)
        ims_l.append(im_s)
        convs_l.append(conv_s)
    k_win_prompt = jnp.stack(kp_l)
    v_win_prompt = jnp.stack(vp_l)
    ssm_re_prompt = jnp.stack(rep_l)
    ssm_im_prompt = jnp.stack(imp_l)
    ffn_conv_prompt = jnp.stack(convp_l)
    k_new_sample = jnp.stack(ks_l)
    v_new_sample = jnp.stack(vs_l)
    ssm_re_sample = jnp.stack(res_l)
    ssm_im_sample = jnp.stack(ims_l)
    ffn_conv_sample = jnp.stack(convs_l)
    return (yp, ys, k_win_prompt, v_win_prompt, ssm_re_prompt, ssm_im_prompt, ffn_conv_prompt,
            k_new_sample, v_new_sample, ssm_re_sample, ssm_im_sample, ffn_conv_sample)
```

```python
import functools
import math

import jax
import jax.numpy as jnp
from jax import lax
from jax.experimental import pallas as pl
from jax.experimental.pallas import tpu as pltpu

F32 = jnp.float32
BF16 = jnp.bfloat16

HEAD_DIM = 128
DILATED_GROUPS = ((128, 1), (512, 4), (2048, 16))
MAX_WINDOW = max(w for w, _ in DILATED_GROUPS)
PAST_LEN = 2048
SSM_GROUP_CH = 16
SSM_STATE = 64
CONV_W = 3
NORM_EPS = 1e-6
MASKED = -1e30

LANES = 128
SUBLANES = 8
MXU_DIM = 256
VMEM_LIMIT_BYTES = 60 * 1024 * 1024

S5_GROUPS_PER_BLOCK = MXU_DIM // SSM_GROUP_CH
S5_BLOCK_CH = S5_GROUPS_PER_BLOCK * SSM_GROUP_CH
S5_BLOCK_STATE = S5_GROUPS_PER_BLOCK * SSM_STATE

ATTN_TQ = 256
FFN_HALO = 16


def _params(*sem):
    return pltpu.CompilerParams(dimension_semantics=sem, vmem_limit_bytes=VMEM_LIMIT_BYTES)


def _rms_rows(x, g):
    ms = jnp.mean(x * x, axis=-1, keepdims=True)
    return x * lax.rsqrt(ms + NORM_EPS) * g


def _in_proj_kernel(x_ref, g_ref, w_ref, hg_ref, o_ref, h_scr, *, n_norm_blocks, heads_per_block):
    j = pl.program_id(1)

    @pl.when(j == 0)
    def _():
        h_scr[...] = _rms_rows(x_ref[...], g_ref[...]).astype(BF16)

    acc = jnp.dot(h_scr[...], w_ref[...], preferred_element_type=F32)

    @pl.when(j < n_norm_blocks)
    def _():
        for hh in range(heads_per_block):
            sl = slice(hh * HEAD_DIM, (hh + 1) * HEAD_DIM)
            o_ref[:, sl] = _rms_rows(acc[:, sl], hg_ref[:, sl])

    @pl.when(j >= n_norm_blocks)
    def _():
        o_ref[...] = acc


def _in_proj(x2d, g1, w_b, head_gain, *, tm, tn):
    T, D = x2d.shape
    N = w_b.shape[1]
    n_norm_blocks = head_gain.shape[1] // tn
    kern = functools.partial(_in_proj_kernel, n_norm_blocks=n_norm_blocks,
                             heads_per_block=tn // HEAD_DIM)
    return pl.pallas_call(
        kern,
        grid=(T // tm, N // tn),
        in_specs=[
            pl.BlockSpec((tm, D), lambda i, j: (i, 0)),
            pl.BlockSpec((1, D), lambda i, j: (0, 0)),
            pl.BlockSpec((D, tn), lambda i, j: (0, j)),
            pl.BlockSpec((1, tn), lambda i, j: (0, jnp.minimum(j, n_norm_blocks - 1))),
        ],
        out_specs=pl.BlockSpec((tm, tn), lambda i, j: (i, j)),
        out_shape=jax.ShapeDtypeStruct((T, N), F32),
        scratch_shapes=[pltpu.VMEM((tm, D), BF16)],
        compiler_params=_params("parallel", "arbitrary"),
        name="in_proj",
    )(x2d, g1, w_b, head_gain)


def _log_multiplicity(d):
    count = jnp.zeros(d.shape, jnp.int32)
    for window, dilation in DILATED_GROUPS:
        count += ((d >= 0) & (d <= window) & (d % dilation == 0)).astype(jnp.int32)
    return jnp.where(count > 0, jnp.log(jnp.maximum(count, 1).astype(F32)), MASKED)


def _alibi_slopes(n_heads):
    return 2.0 ** (-8.0 * jnp.arange(1, n_heads + 1, dtype=F32) / n_heads)


def _attn_prompt_kernel(slopes_ref, q_ref, k_ref, v_ref, logc_ref, o_ref, kb_scr, vb_scr, bias_scr,
                        *, n_off):
    h = pl.program_id(1)
    qi = pl.program_id(2)
    tq = q_ref.shape[0]

    @pl.when(qi == 0)
    def _():
        kb_scr[...] = k_ref[...].astype(BF16)
        vb_scr[...] = v_ref[...].astype(BF16)
        slope = slopes_ref[h]
        row = lax.broadcasted_iota(jnp.int32, (tq, tq), 0)
        col = lax.broadcasted_iota(jnp.int32, (tq, tq), 1)
        base = (row - col).astype(F32)
        for m in range(n_off):
            bias_scr[m] = logc_ref[m] - slope * (base + float(m * tq))

    q = q_ref[...].astype(BF16)
    scale = HEAD_DIM ** -0.5

    def body(m, carry):
        m_i, l_i, acc = carry
        start = pl.multiple_of((qi - m) * tq, tq)
        kb = kb_scr[pl.ds(start, tq), :]
        vb = vb_scr[pl.ds(start, tq), :]
        s = lax.dot_general(q, kb, (((1,), (1,)), ((), ())), preferred_element_type=F32)
        s = s * scale + bias_scr[m]
        m_new = jnp.maximum(m_i, jnp.max(s, axis=-1, keepdims=True))
        alpha = jnp.exp(m_i - m_new)
        p = jnp.exp(s - m_new)
        l_new = alpha * l_i + jnp.sum(p, axis=-1, keepdims=True)
        acc = alpha * acc + jnp.dot(p.astype(BF16), vb, preferred_element_type=F32)
        return m_new, l_new, acc

    init = (jnp.full((tq, 1), MASKED, F32), jnp.zeros((tq, 1), F32), jnp.zeros((tq, HEAD_DIM), F32))
    n_blocks = jnp.minimum(qi, n_off - 1) + 1
    _, l_i, acc = lax.fori_loop(0, n_blocks, body, init)
    o_ref[...] = (acc / l_i).astype(o_ref.dtype)


def _attn_prompt(proj, n_batch, seq, n_heads):
    tq = ATTN_TQ
    nq = seq // tq
    n_off = MAX_WINDOW // tq + 1
    m = jnp.arange(n_off, dtype=jnp.int32)[:, None, None]
    r = jnp.arange(tq, dtype=jnp.int32)[None, :, None]
    c = jnp.arange(tq, dtype=jnp.int32)[None, None, :]
    logc = _log_multiplicity(m * tq + r - c)
    kern = functools.partial(_attn_prompt_kernel, n_off=n_off)
    return pl.pallas_call(
        kern,
        grid=(n_batch, n_heads, nq),
        in_specs=[
            pl.BlockSpec(memory_space=pltpu.SMEM),
            pl.BlockSpec((tq, HEAD_DIM), lambda b, h, i: (b * nq + i, h)),
            pl.BlockSpec((seq, HEAD_DIM), lambda b, h, i: (b, n_heads + h)),
            pl.BlockSpec((seq, HEAD_DIM), lambda b, h, i: (b, 2 * n_heads + h)),
            pl.BlockSpec((n_off, tq, tq), lambda b, h, i: (0, 0, 0)),
        ],
        out_specs=pl.BlockSpec((tq, HEAD_DIM), lambda b, h, i: (b * nq + i, h)),
        out_shape=jax.ShapeDtypeStruct((n_batch * seq, n_heads * HEAD_DIM), BF16),
        scratch_shapes=[pltpu.VMEM((seq, HEAD_DIM), BF16), pltpu.VMEM((seq, HEAD_DIM), BF16),
                        pltpu.VMEM((n_off, tq, tq), F32)],
        compiler_params=_params("parallel", "parallel", "arbitrary"),
        name="attn_prompt",
    )(_alibi_slopes(n_heads), proj, proj, proj, logc)


def _attn_sample_kernel(q_ref, kn_ref, vn_ref, kc_ref, vc_ref, bc_ref, bn_ref, o_ref, *, heads):
    scale = HEAD_DIM ** -0.5
    n_new = q_ref.shape[0]
    pad = jnp.zeros((LANES - n_new, HEAD_DIM), F32)
    nt = (((1,), (1,)), ((), ()))
    for hh in range(heads):
        sl = slice(hh * HEAD_DIM, (hh + 1) * HEAD_DIM)
        q = q_ref[:, sl].astype(BF16)
        kc = kc_ref[:, sl].astype(BF16)
        vc = vc_ref[:, sl].astype(BF16)
        kn = jnp.concatenate([kn_ref[:, sl], pad], axis=0).astype(BF16)
        vn = jnp.concatenate([vn_ref[:, sl], pad], axis=0).astype(BF16)
        s_c = lax.dot_general(q, kc, nt, preferred_element_type=F32) * scale + bc_ref[hh]
        s_n = lax.dot_general(q, kn, nt, preferred_element_type=F32) * scale + bn_ref[hh]
        m = jnp.maximum(jnp.max(s_c, axis=-1, keepdims=True), jnp.max(s_n, axis=-1, keepdims=True))
        p_c = jnp.exp(s_c - m)
        p_n = jnp.exp(s_n - m)
        l = jnp.sum(p_c, axis=-1, keepdims=True) + jnp.sum(p_n, axis=-1, keepdims=True)
        o = (jnp.dot(p_c.astype(BF16), vc, preferred_element_type=F32)
             + jnp.dot(p_n.astype(BF16), vn, preferred_element_type=F32))
        o_ref[:, sl] = o / l


def _attn_sample(proj, cache_k, cache_v, n_heads, *, heads_per_step=4):
    n_seq, win = cache_k.shape[0], cache_k.shape[1]
    n_new = proj.shape[0] // n_seq
    d_att = n_heads * HEAD_DIM
    kc = cache_k.reshape(n_seq, win, d_att)
    vc = cache_v.reshape(n_seq, win, d_att)
    slopes = _alibi_slopes(n_heads)[:, None, None]
    i = jnp.arange(n_new, dtype=jnp.int32)[:, None]
    d_c = win + i - jnp.arange(win, dtype=jnp.int32)[None, :]
    bias_c = _log_multiplicity(d_c)[None] - slopes * d_c.astype(F32)[None]
    j = jnp.arange(LANES, dtype=jnp.int32)[None, :]
    d_n = jnp.where(j < n_new, i - j, -1)
    bias_n = _log_multiplicity(d_n)[None] - slopes * jnp.maximum(d_n, 0).astype(F32)[None]
    hs = heads_per_step
    wb = hs * HEAD_DIM
    nhg = n_heads // hs
    kern = functools.partial(_attn_sample_kernel, heads=hs)
    return pl.pallas_call(
        kern,
        grid=(n_seq, nhg),
        in_specs=[
            pl.BlockSpec((n_new, wb), lambda b, g: (b, g)),
            pl.BlockSpec((n_new, wb), lambda b, g: (b, nhg + g)),
            pl.BlockSpec((n_new, wb), lambda b, g: (b, 2 * nhg + g)),
            pl.BlockSpec((None, win, wb), lambda b, g: (b, 0, g)),
            pl.BlockSpec((None, win, wb), lambda b, g: (b, 0, g)),
            pl.BlockSpec((hs, n_new, win), lambda b, g: (g, 0, 0)),
            pl.BlockSpec((hs, n_new, LANES), lambda b, g: (g, 0, 0)),
        ],
        out_specs=pl.BlockSpec((n_new, wb), lambda b, g: (b, g)),
        out_shape=jax.ShapeDtypeStruct((n_seq * n_new, d_att), F32),
        compiler_params=_params("parallel", "arbitrary"),
        name="attn_sample",
    )(proj, proj, proj, kc, vc, bias_c, bias_n)


def _s5_pack(a_re, a_im, log_dt, b_re, b_im, c_re, c_im):
    dt = jnp.exp(log_dt.astype(F32))[:, None]
    ar, ai = a_re.astype(F32), a_im.astype(F32)
    mag = jnp.exp(ar * dt)
    lb_re, lb_im = mag * jnp.cos(ai * dt), mag * jnp.sin(ai * dt)
    den = ar * ar + ai * ai
    ir, ii = ar / den, -ai / den
    cr = (lb_re - 1.0) * ir - lb_im * ii
    ci = (lb_re - 1.0) * ii + lb_im * ir
    bb_re = cr[..., None] * b_re - ci[..., None] * b_im
    bb_im = cr[..., None] * b_im + ci[..., None] * b_re
    n_blk = a_re.shape[0] // S5_GROUPS_PER_BLOCK
    gb, p, ch = S5_GROUPS_PER_BLOCK, SSM_STATE, SSM_GROUP_CH
    eye = jnp.eye(gb, dtype=F32)

    def pack_in(bb):
        return jnp.einsum('aGpc,GH->aGcHp', bb.reshape(n_blk, gb, p, ch), eye).reshape(n_blk, gb * ch, gb * p)

    def pack_out(cc):
        return jnp.einsum('aGcp,GH->aHpGc', cc.reshape(n_blk, gb, ch, p), eye).reshape(n_blk, gb * p, gb * ch)

    b_blk = jnp.concatenate([pack_in(bb_re), pack_in(bb_im)], axis=2).astype(BF16)
    c_blk = jnp.concatenate([pack_out(c_re.astype(F32)), -pack_out(c_im.astype(F32))], axis=1).astype(BF16)
    lbr = lb_re.reshape(n_blk, gb * p)
    lbi = lb_im.reshape(n_blk, gb * p)
    return b_blk, c_blk, lbr, lbi


def _s5_prompt_kernel(u_ref, b_ref, c_ref, lbr_ref, lbi_ref, d_ref, z_ref, sre_ref, sim_ref,
                      s_scr, xr_scr, xi_scr):
    i = pl.program_id(1)
    lc = u_ref.shape[0]
    n_blk = b_ref.shape[0]
    n_slab = s_scr.shape[0]
    half = n_slab // 2

    @pl.when(i == 0)
    def _():
        xr_scr[...] = jnp.zeros_like(xr_scr)
        xi_scr[...] = jnp.zeros_like(xi_scr)

    u = u_ref[...]
    ub = u.astype(BF16)
    for g in range(n_blk):
        bu = jnp.dot(ub[:, g * S5_BLOCK_CH:(g + 1) * S5_BLOCK_CH], b_ref[g], preferred_element_type=F32)
        for s in range(n_slab):
            s_scr[s, pl.ds(g, lc, stride=n_blk), :] = bu[:, s * LANES:(s + 1) * LANES]

    def step(t, carry):
        row = pl.multiple_of(t * n_blk, n_blk)
        new = []
        for s in range(half):
            xr, xi = carry[s], carry[half + s]
            lr = lbr_ref[:, s * LANES:(s + 1) * LANES]
            li = lbi_ref[:, s * LANES:(s + 1) * LANES]
            nr = lr * xr - li * xi + s_scr[s, pl.ds(row, n_blk), :]
            ni = lr * xi + li * xr + s_scr[half + s, pl.ds(row, n_blk), :]
            s_scr[s, pl.ds(row, n_blk), :] = nr
            s_scr[half + s, pl.ds(row, n_blk), :] = ni
            new.append((nr, ni))
        return tuple(n[0] for n in new) + tuple(n[1] for n in new)

    init = tuple(xr_scr[:, s * LANES:(s + 1) * LANES] for s in range(half)) + \
        tuple(xi_scr[:, s * LANES:(s + 1) * LANES] for s in range(half))
    fin = lax.fori_loop(0, lc, step, init, unroll=4)
    for s in range(half):
        xr_scr[:, s * LANES:(s + 1) * LANES] = fin[s]
        xi_scr[:, s * LANES:(s + 1) * LANES] = fin[half + s]
    sre_ref[...] = xr_scr[...]
    sim_ref[...] = xi_scr[...]

    for g in range(n_blk):
        xs = jnp.concatenate([s_scr[s, pl.ds(g, lc, stride=n_blk), :] for s in range(n_slab)], axis=1)
        y = jnp.dot(xs.astype(BF16), c_ref[g], preferred_element_type=F32)
        sl = slice(g * S5_BLOCK_CH, (g + 1) * S5_BLOCK_CH)
        y = y + d_ref[:, sl] * u[:, sl]
        z_ref[:, sl] = jax.nn.gelu(y)


def _s5_prompt(proj, n_batch, seq, u_col_block, b_blk, c_blk, lbr, lbi, d_skip, *, lc=128):
    n_blk = b_blk.shape[0]
    assert n_blk == SUBLANES
    d_ssm = n_blk * S5_BLOCK_CH
    nst = S5_BLOCK_STATE
    nc = seq // lc
    n_slab = 2 * nst // LANES
    z, sre, sim = pl.pallas_call(
        _s5_prompt_kernel,
        grid=(n_batch, nc),
        in_specs=[
            pl.BlockSpec((lc, d_ssm), lambda b, i: (b * nc + i, u_col_block)),
            pl.BlockSpec(b_blk.shape, lambda b, i: (0, 0, 0)),
            pl.BlockSpec(c_blk.shape, lambda b, i: (0, 0, 0)),
            pl.BlockSpec(lbr.shape, lambda b, i: (0, 0)),
            pl.BlockSpec(lbi.shape, lambda b, i: (0, 0)),
            pl.BlockSpec((1, d_ssm), lambda b, i: (0, 0)),
        ],
        out_specs=[
            pl.BlockSpec((lc, d_ssm), lambda b, i: (b * nc + i, 0)),
            pl.BlockSpec((None, n_blk, nst), lambda b, i: (b, 0, 0)),
            pl.BlockSpec((None, n_blk, nst), lambda b, i: (b, 0, 0)),
        ],
        out_shape=[
            jax.ShapeDtypeStruct((n_batch * seq, d_ssm), F32),
            jax.ShapeDtypeStruct((n_batch, n_blk, nst), F32),
            jax.ShapeDtypeStruct((n_batch, n_blk, nst), F32),
        ],
        scratch_shapes=[pltpu.VMEM((n_slab, lc * n_blk, LANES), F32),
                        pltpu.VMEM((n_blk, nst), F32), pltpu.VMEM((n_blk, nst), F32)],
        compiler_params=_params("parallel", "arbitrary"),
        name="s5_prompt",
    )(proj, b_blk, c_blk, lbr, lbi, d_skip)
    return z, sre, sim


def _s5_sample_kernel(u_ref, b_ref, c_ref, lbr_ref, lbi_ref, d_ref, x0r_ref, x0i_ref,
                      z_ref, xr_ref, xi_ref):
    n_steps = u_ref.shape[0]
    lr = lbr_ref[...]
    li = lbi_ref[...]
    xr = x0r_ref[...]
    xi = x0i_ref[...]
    nst = xr.shape[1]
    for t in range(n_steps):
        u = u_ref[t]
        bu = jnp.dot(u.astype(BF16), b_ref[...], preferred_element_type=F32)
        xr, xi = lr * xr - li * xi + bu[:, :nst], lr * xi + li * xr + bu[:, nst:]
        xs = jnp.concatenate([xr, xi], axis=1).astype(BF16)
        y = jnp.dot(xs, c_ref[...], preferred_element_type=F32) + d_ref[...] * u
        z_ref[t] = jax.nn.gelu(y)
    xr_ref[...] = xr
    xi_ref[...] = xi


def _s5_sample(u_tm, b_blk, c_blk, lbr, lbi, d_skip, x0r, x0i):
    n_steps, n_seq, d_ssm = u_tm.shape
    n_blk = b_blk.shape[0]
    nst = S5_BLOCK_STATE
    bc = S5_BLOCK_CH
    return pl.pallas_call(
        _s5_sample_kernel,
        grid=(n_blk,),
        in_specs=[
            pl.BlockSpec((n_steps, n_seq, bc), lambda g: (0, 0, g)),
            pl.BlockSpec((None, bc, 2 * nst), lambda g: (g, 0, 0)),
            pl.BlockSpec((None, 2 * nst, bc), lambda g: (g, 0, 0)),
            pl.BlockSpec((None, 1, nst), lambda g: (g, 0, 0)),
            pl.BlockSpec((None, 1, nst), lambda g: (g, 0, 0)),
            pl.BlockSpec((1, bc), lambda g: (0, g)),
            pl.BlockSpec((n_seq, nst), lambda g: (0, g)),
            pl.BlockSpec((n_seq, nst), lambda g: (0, g)),
        ],
        out_specs=[
            pl.BlockSpec((n_steps, n_seq, bc), lambda g: (0, 0, g)),
            pl.BlockSpec((n_seq, nst), lambda g: (0, g)),
            pl.BlockSpec((n_seq, nst), lambda g: (0, g)),
        ],
        out_shape=[
            jax.ShapeDtypeStruct((n_steps, n_seq, d_ssm), F32),
            jax.ShapeDtypeStruct(x0r.shape, F32),
            jax.ShapeDtypeStruct(x0i.shape, F32),
        ],
        compiler_params=_params("parallel"),
        name="s5_sample",
    )(u_tm, b_blk, c_blk, lbr.reshape(n_blk, 1, nst), lbi.reshape(n_blk, 1, nst), d_skip, x0r, x0i)


def _glu_kernel(z_ref, w_ref, b_ref, o_ref):
    z = z_ref[...]
    gate = jnp.dot(z.astype(BF16), w_ref[...], preferred_element_type=F32) + b_ref[...]
    o_ref[...] = (z * jax.nn.sigmoid(gate)).astype(o_ref.dtype)


def _glu(z, w_b, b, *, tm):
    T, N = z.shape
    return pl.pallas_call(
        _glu_kernel,
        grid=(T // tm,),
        in_specs=[
            pl.BlockSpec((tm, N), lambda i: (i, 0)),
            pl.BlockSpec((N, N), lambda i: (0, 0)),
            pl.BlockSpec((1, N), lambda i: (0, 0)),
        ],
        out_specs=pl.BlockSpec((tm, N), lambda i: (i, 0)),
        out_shape=jax.ShapeDtypeStruct((T, N), BF16),
        compiler_params=_params("parallel"),
        name="glu",
    )(z, w_b, b)


def _out_proj_kernel(a1_ref, a2_ref, w1_ref, w2_ref, x_ref, o_ref):
    acc = jnp.dot(a1_ref[...], w1_ref[...], preferred_element_type=F32)
    acc = acc + jnp.dot(a2_ref[...], w2_ref[...], preferred_element_type=F32)
    o_ref[...] = x_ref[...] + acc


def _out_proj(o_att, o_ssm, w_b, x2d, *, tm, tn):
    T, D = x2d.shape
    k1, k2 = o_att.shape[1], o_ssm.shape[1]
    assert k1 == k2
    return pl.pallas_call(
        _out_proj_kernel,
        grid=(T // tm, D // tn),
        in_specs=[
            pl.BlockSpec((tm, k1), lambda i, j: (i, 0)),
            pl.BlockSpec((tm, k2), lambda i, j: (i, 0)),
            pl.BlockSpec((k1, tn), lambda i, j: (0, j)),
            pl.BlockSpec((k2, tn), lambda i, j: (1, j)),
            pl.BlockSpec((tm, tn), lambda i, j: (i, j)),
        ],
        out_specs=pl.BlockSpec((tm, tn), lambda i, j: (i, j)),
        out_shape=jax.ShapeDtypeStruct((T, D), F32),
        compiler_params=_params("parallel", "arbitrary"),
        name="out_proj",
    )(o_att, o_ssm, w_b, w_b, x2d)


def _ffn_kernel(*refs, seq_len, has_state):
    if has_state:
        (x_ref, halo_ref, g_ref, wg_ref, wu_ref, wd_ref, cw_ref, cb_ref, e1_ref, e2_ref,
         y_ref, a_ref, h_scr) = refs
    else:
        (x_ref, halo_ref, g_ref, wg_ref, wu_ref, wd_ref, cw_ref, cb_ref,
         y_ref, a_ref, h_scr) = refs
    i = pl.program_id(0)
    f = pl.program_id(1)
    tm = x_ref.shape[0]
    tf = wg_ref.shape[1]

    @pl.when(f == 0)
    def _():
        x = x_ref[...]
        h_scr[0:FFN_HALO, :] = _rms_rows(halo_ref[...], g_ref[...]).astype(BF16)
        h_scr[FFN_HALO:, :] = _rms_rows(x, g_ref[...]).astype(BF16)
        y_ref[...] = x

    a_ext = jnp.dot(h_scr[...], wg_ref[...], preferred_element_type=F32)
    a = a_ext[FFN_HALO:]
    pos = (i * tm + lax.broadcasted_iota(jnp.int32, (tm, tf), 0)) % seq_len
    prev1 = a_ext[FFN_HALO - 1:FFN_HALO - 1 + tm]
    prev2 = a_ext[FFN_HALO - 2:FFN_HALO - 2 + tm]
    if has_state:
        prev1 = jnp.where(pos >= 1, prev1, e1_ref[...])
        prev2 = jnp.where(pos >= 2, prev2, e2_ref[...])
    else:
        prev1 = jnp.where(pos >= 1, prev1, 0.0)
        prev2 = jnp.where(pos >= 2, prev2, 0.0)
    conv = cb_ref[...] + cw_ref[0:1, :] * prev2 + cw_ref[1:2, :] * prev1 + cw_ref[2:3, :] * a
    up = jnp.dot(h_scr[FFN_HALO:, :], wu_ref[...], preferred_element_type=F32)
    gated = (jax.nn.silu(conv) * up).astype(BF16)
    y_ref[...] += jnp.dot(gated, wd_ref[...], preferred_element_type=F32)
    if has_state:
        a_ref[...] = a
    else:
        a_ref[...] = a[tm - SUBLANES:]


def _ffn(x1, g2, wg_b, wu_b, wd_b, conv_w, conv_b, seq_len, e1=None, e2=None, *, tm, tf):
    T, D = x1.shape
    F = wg_b.shape[1]
    has_state = e1 is not None
    assert seq_len % tm == 0 or tm % seq_len == 0
    blocks_per_halo = tm // FFN_HALO
    in_specs = [
        pl.BlockSpec((tm, D), lambda i, f: (i, 0), pipeline_mode=pl.Buffered(1)),
        pl.BlockSpec((FFN_HALO, D), lambda i, f: (jnp.maximum(i * blocks_per_halo - 1, 0), 0)),
        pl.BlockSpec((1, D), lambda i, f: (0, 0)),
        pl.BlockSpec((D, tf), lambda i, f: (0, f)),
        pl.BlockSpec((D, tf), lambda i, f: (0, f)),
        pl.BlockSpec((tf, D), lambda i, f: (f, 0)),
        pl.BlockSpec((CONV_W, tf), lambda i, f: (0, f)),
        pl.BlockSpec((1, tf), lambda i, f: (0, f)),
    ]
    args = [x1, x1, g2, wg_b, wu_b, wd_b, conv_w, conv_b]
    if has_state:
        in_specs += [pl.BlockSpec((tm, tf), lambda i, f: (i, f))] * 2
        args += [e1, e2]
        a_spec = pl.BlockSpec((tm, tf), lambda i, f: (i, f))
        a_shape = jax.ShapeDtypeStruct((T, F), F32)
    else:
        a_spec = pl.BlockSpec((SUBLANES, tf), lambda i, f: (i, f))
        a_shape = jax.ShapeDtypeStruct((T // tm * SUBLANES, F), F32)
    kern = functools.partial(_ffn_kernel, seq_len=seq_len, has_state=has_state)
    return pl.pallas_call(
        kern,
        grid=(T // tm, F // tf),
        in_specs=in_specs,
        out_specs=[pl.BlockSpec((tm, D), lambda i, f: (i, 0)), a_spec],
        out_shape=[jax.ShapeDtypeStruct((T, D), F32), a_shape],
        scratch_shapes=[pltpu.VMEM((FFN_HALO + tm, D), BF16)],
        compiler_params=_params("parallel", "arbitrary"),
        name="ffn",
    )(*args)


def _layer(x3d, lw, *, cache=None, ssm0=None, conv0=None):
    (norm1_g, w_in, q_norm_g, k_norm_g, a_re, a_im, log_dt, b_re, b_im, c_re, c_im, ssm_d,
     w_glu, b_glu, w_out, norm2_g, w_gate, w_up, conv_w, conv_b, w_down) = lw
    nb, L, D = x3d.shape
    T = nb * L
    x2d = x3d.reshape(T, D)
    d_ssm = ssm_d.shape[0]
    d_att = (w_in.shape[1] - d_ssm) // 3
    n_heads = d_att // HEAD_DIM
    n_groups = a_re.shape[0]
    tm = min(512, T)

    head_gain = jnp.concatenate([jnp.tile(q_norm_g.astype(F32), n_heads),
                                 jnp.tile(k_norm_g.astype(F32), n_heads)])[None]
    proj = _in_proj(x2d, norm1_g[None], w_in.astype(BF16), head_gain, tm=tm, tn=1024)
    k = proj[:, d_att:2 * d_att].reshape(nb, L, n_heads, HEAD_DIM)
    v = proj[:, 2 * d_att:3 * d_att].reshape(nb, L, n_heads, HEAD_DIM)

    b_blk, c_blk, lbr, lbi = _s5_pack(a_re, a_im, log_dt, b_re, b_im, c_re, c_im)
    d_skip = ssm_d.astype(F32)[None]
    if cache is None:
        o_att = _attn_prompt(proj, nb, L, n_heads)
        z, sre, sim = _s5_prompt(proj, nb, L, 3 * d_att // d_ssm, b_blk, c_blk, lbr, lbi, d_skip)
        ssm_re = sre.reshape(nb, n_groups, SSM_STATE)
        ssm_im = sim.reshape(nb, n_groups, SSM_STATE)
    else:
        o_att = _attn_sample(proj, cache[0], cache[1], n_heads).astype(BF16)
        u_tm = proj[:, 3 * d_att:].reshape(nb, L, d_ssm).transpose(1, 0, 2)
        z_tm, sre, sim = _s5_sample(u_tm, b_blk, c_blk, lbr, lbi, d_skip,
                                    ssm0[0].astype(F32).reshape(nb, n_groups * SSM_STATE),
                                    ssm0[1].astype(F32).reshape(nb, n_groups * SSM_STATE))
        z = z_tm.transpose(1, 0, 2).reshape(T, d_ssm)
        ssm_re = sre.reshape(nb, n_groups, SSM_STATE)
        ssm_im = sim.reshape(nb, n_groups, SSM_STATE)

    o_ssm = _glu(z, w_glu.astype(BF16), b_glu.astype(F32)[None], tm=tm)
    x1 = _out_proj(o_att, o_ssm, w_out.astype(BF16), x2d, tm=tm, tn=1024)

    ffn_w = (norm2_g[None], w_gate.astype(BF16), w_up.astype(BF16), w_down.astype(BF16),
             conv_w.astype(F32), conv_b.astype(F32)[None])
    F = w_gate.shape[1]
    if conv0 is None:
        y, a_tail = _ffn(x1, *ffn_w, L, tm=tm, tf=256)
        tiles_per_seq = L // tm
        a_tail = a_tail.reshape(nb, tiles_per_seq, SUBLANES, F)
        conv_state = a_tail[:, -1, SUBLANES - (CONV_W - 1):]
    else:
        c0 = conv0.astype(F32)
        zeros = jnp.zeros((nb, L - 1, F), F32)
        e1 = jnp.concatenate([c0[:, 1:2], zeros], axis=1).reshape(T, F)
        e2 = jnp.concatenate([c0[:, 0:1], c0[:, 1:2], zeros[:, 1:]], axis=1).reshape(T, F)
        y, a_full = _ffn(x1, *ffn_w, L, e1, e2, tm=tm, tf=256)
        conv_state = a_full.reshape(nb, L, F)[:, L - (CONV_W - 1):]
    return y.reshape(nb, L, D), k, v, ssm_re, ssm_im, conv_state


def kernel(x_prompt, x_sample, cache_k, cache_v, state_ssm_re, state_ssm_im, state_ffn_conv,
           norm1_g, w_in, q_norm_g, k_norm_g,
           ssm_a_re, ssm_a_im, ssm_log_dt, ssm_b_re, ssm_b_im, ssm_c_re, ssm_c_im,
           ssm_d, w_glu, b_glu, w_out, norm2_g,
           w_ffn_gate, w_ffn_up, ffn_conv_w, ffn_conv_b, w_ffn_down):
    depth = w_in.shape[0]
    seq = x_prompt.shape[1]
    wb_prompt = min(MAX_WINDOW, seq)
    outs_p = [[] for _ in range(5)]
    outs_s = [[] for _ in range(5)]
    yp, ys = x_prompt, x_sample
    for l in range(depth):
        lw = (norm1_g[l], w_in[l], q_norm_g[l], k_norm_g[l],
              ssm_a_re[l], ssm_a_im[l], ssm_log_dt[l], ssm_b_re[l], ssm_b_im[l], ssm_c_re[l], ssm_c_im[l],
              ssm_d[l], w_glu[l], b_glu[l], w_out[l], norm2_g[l],
              w_ffn_gate[l], w_ffn_up[l], ffn_conv_w[l], ffn_conv_b[l], w_ffn_down[l])
        yp, k_p, v_p, re_p, im_p, conv_p = _layer(yp, lw)
        ys, k_s, v_s, re_s, im_s, conv_s = _layer(
            ys, lw, cache=(cache_k[l], cache_v[l]),
            ssm0=(state_ssm_re[l], state_ssm_im[l]), conv0=state_ffn_conv[l])
        for lst, val in zip(outs_p, (k_p[:, seq - wb_prompt:], v_p[:, seq - wb_prompt:], re_p, im_p, conv_p)):
            lst.append(val)
        for lst, val in zip(outs_s, (k_s, v_s, re_s, im_s, conv_s)):
            lst.append(val)
    return (yp, ys, *(jnp.stack(o) for o in outs_p), *(jnp.stack(o) for o in outs_s))
```

```python
import functools

import jax
import jax.numpy as jnp
import numpy as np
from jax import lax
from jax.experimental import pallas as pl
from jax.experimental.pallas import tpu as pltpu

F32 = jnp.float32
BF16 = jnp.bfloat16

HEAD_DIM = 128
DILATED_GROUPS = ((128, 1), (512, 4), (2048, 16))
MAX_WINDOW = max(w for w, _ in DILATED_GROUPS)
SSM_GROUP_CH = 16
SSM_STATE = 64
CONV_W = 3
NORM_EPS = 1e-6
MASKED = -1e30

LANES = 128
SUBLANES = 8
MXU_DIM = 256
VMEM_LIMIT_BYTES = 60 * 1024 * 1024

S5_GROUPS_PER_BLOCK = MXU_DIM // SSM_GROUP_CH
S5_BLOCK_CH = S5_GROUPS_PER_BLOCK * SSM_GROUP_CH
S5_BLOCK_STATE = S5_GROUPS_PER_BLOCK * SSM_STATE

ATTN_TQ = 256
PROJ_TM = 1024
FFN_TM = 512
FFN_TF = 512
FFN_HALO = 16


def _params(*sem):
    return pltpu.CompilerParams(dimension_semantics=sem, vmem_limit_bytes=VMEM_LIMIT_BYTES)


def _rms_rows(x, g):
    ms = jnp.mean(x * x, axis=-1, keepdims=True)
    return x * lax.rsqrt(ms + NORM_EPS) * g


def _rms_rows_to(dst_ref, dst_row0, src_ref, g_ref, chunk=128):
    rows = src_ref.shape[0]
    chunk = min(chunk, rows)

    def body(c, carry):
        r0 = pl.multiple_of(c * chunk, chunk)
        dst_ref[pl.ds(dst_row0 + r0, chunk), :] = _rms_rows(src_ref[pl.ds(r0, chunk), :], g_ref[...]).astype(BF16)
        return carry

    lax.fori_loop(0, rows // chunk, body, 0)


def _in_proj_kernel(x_ref, g_ref, w_ref, hg_ref, o_ref, h_scr, *, n_norm_blocks, heads_per_block):
    j = pl.program_id(1)

    @pl.when(j == 0)
    def _():
        _rms_rows_to(h_scr, 0, x_ref, g_ref)

    acc = jnp.dot(h_scr[...], w_ref[...], preferred_element_type=F32)

    @pl.when(j < n_norm_blocks)
    def _():
        for hh in range(heads_per_block):
            sl = slice(hh * HEAD_DIM, (hh + 1) * HEAD_DIM)
            o_ref[:, sl] = _rms_rows(acc[:, sl], hg_ref[:, sl])

    @pl.when(j >= n_norm_blocks)
    def _():
        o_ref[...] = acc


def _in_proj(x2d, g1, w_b, head_gain, *, tm, tn):
    T, D = x2d.shape
    N = w_b.shape[1]
    n_norm_blocks = head_gain.shape[1] // tn
    kern = functools.partial(_in_proj_kernel, n_norm_blocks=n_norm_blocks,
                             heads_per_block=tn // HEAD_DIM)
    return pl.pallas_call(
        kern,
        grid=(T // tm, N // tn),
        in_specs=[
            pl.BlockSpec((tm, D), lambda i, j: (i, 0), pipeline_mode=pl.Buffered(1)),
            pl.BlockSpec((1, D), lambda i, j: (0, 0)),
            pl.BlockSpec((D, tn), lambda i, j: (0, j)),
            pl.BlockSpec((1, tn), lambda i, j: (0, jnp.minimum(j, n_norm_blocks - 1))),
        ],
        out_specs=pl.BlockSpec((tm, tn), lambda i, j: (i, j)),
        out_shape=jax.ShapeDtypeStruct((T, N), F32),
        scratch_shapes=[pltpu.VMEM((tm, D), BF16)],
        compiler_params=_params("parallel", "arbitrary"),
        name="in_proj",
    )(x2d, g1, w_b, head_gain)


def _multiplicity(d):
    count = 0
    for window, dilation in DILATED_GROUPS:
        count = count + ((d >= 0) & (d <= window) & (d % dilation == 0)).astype(jnp.int32)
    return count


def _log_multiplicity(d):
    count = _multiplicity(d)
    return jnp.where(count > 0, jnp.log(jnp.maximum(count, 1).astype(F32)), MASKED)


def _alibi_slopes(n_heads):
    return 2.0 ** (-8.0 * jnp.arange(1, n_heads + 1, dtype=F32) / n_heads)


def _attn_prompt_kernel(slopes_ref, q_ref, k_ref, v_ref, logc_ref, o_ref, kb_scr, vb_scr, bias_scr,
                        *, n_off):
    h = pl.program_id(1)
    qi = pl.program_id(2)
    tq = q_ref.shape[0]
    pad = (n_off - 1) * tq
    wlen = n_off * tq

    @pl.when(qi == 0)
    def _():
        kb_scr[0:pad, :] = jnp.zeros((pad, HEAD_DIM), BF16)
        vb_scr[0:pad, :] = jnp.zeros((pad, HEAD_DIM), BF16)
        kb_scr[pad:, :] = k_ref[...].astype(BF16)
        vb_scr[pad:, :] = v_ref[...].astype(BF16)
        slope = slopes_ref[h]
        row = lax.broadcasted_iota(jnp.int32, (tq, tq), 0)
        col = lax.broadcasted_iota(jnp.int32, (tq, tq), 1)
        base = (row - col).astype(F32)
        for j in range(n_off):
            m = n_off - 1 - j
            bias_scr[:, j * tq:(j + 1) * tq] = logc_ref[m] - slope * (base + float(m * tq))

    q = q_ref[...].astype(BF16)
    start = pl.multiple_of(qi * tq, tq)
    kw = kb_scr[pl.ds(start, wlen), :]
    vw = vb_scr[pl.ds(start, wlen), :]
    s = lax.dot_general(q, kw, (((1,), (1,)), ((), ())), preferred_element_type=F32)
    lane = lax.broadcasted_iota(jnp.int32, (1, wlen), 1)
    before_start = jnp.where(lane < (n_off - 1 - qi) * tq, MASKED, 0.0)
    s = s * (HEAD_DIM ** -0.5) + bias_scr[...] + before_start
    m_i = jnp.max(s, axis=-1, keepdims=True)
    p = jnp.exp(s - m_i)
    l_i = jnp.sum(p, axis=-1, keepdims=True)
    acc = jnp.dot(p.astype(BF16), vw, preferred_element_type=F32)
    o_ref[...] = (acc / l_i).astype(o_ref.dtype)


def _attn_prompt(proj, n_batch, seq, n_heads):
    tq = ATTN_TQ
    nq = seq // tq
    n_off = MAX_WINDOW // tq + 1
    m = jnp.arange(n_off, dtype=jnp.int32)[:, None, None]
    r = jnp.arange(tq, dtype=jnp.int32)[None, :, None]
    c = jnp.arange(tq, dtype=jnp.int32)[None, None, :]
    logc = _log_multiplicity(m * tq + r - c)
    kern = functools.partial(_attn_prompt_kernel, n_off=n_off)
    pad = (n_off - 1) * tq
    return pl.pallas_call(
        kern,
        grid=(n_batch, n_heads, nq),
        in_specs=[
            pl.BlockSpec(memory_space=pltpu.SMEM),
            pl.BlockSpec((tq, HEAD_DIM), lambda b, h, i: (b * nq + i, h)),
            pl.BlockSpec((seq, HEAD_DIM), lambda b, h, i: (b, n_heads + h)),
            pl.BlockSpec((seq, HEAD_DIM), lambda b, h, i: (b, 2 * n_heads + h)),
            pl.BlockSpec((n_off, tq, tq), lambda b, h, i: (0, 0, 0)),
        ],
        out_specs=pl.BlockSpec((tq, HEAD_DIM), lambda b, h, i: (b * nq + i, h)),
        out_shape=jax.ShapeDtypeStruct((n_batch * seq, n_heads * HEAD_DIM), BF16),
        scratch_shapes=[pltpu.VMEM((pad + seq, HEAD_DIM), BF16), pltpu.VMEM((pad + seq, HEAD_DIM), BF16),
                        pltpu.VMEM((tq, n_off * tq), F32)],
        compiler_params=_params("parallel", "parallel", "arbitrary"),
        name="attn_prompt",
    )(_alibi_slopes(n_heads), proj, proj, proj, logc)


def _attn_sample_kernel(q_ref, kn_ref, vn_ref, kf_ref, kr_ref, vf_ref, vr_ref,
                        bf_ref, br_ref, bn_ref, o_ref, *, heads):
    n_new = q_ref.shape[0]
    nt = (((1,), (1,)), ((), ()))
    scale = HEAD_DIM ** -0.5

    def by_head(ref):
        return jnp.concatenate([ref[:, h * HEAD_DIM:(h + 1) * HEAD_DIM] for h in range(heads)], axis=0)

    q = by_head(q_ref).astype(BF16)
    pad = jnp.zeros((LANES - heads * n_new, HEAD_DIM), F32)
    kn = jnp.concatenate([by_head(kn_ref), pad], axis=0).astype(BF16)
    vn = jnp.concatenate([by_head(vn_ref), pad], axis=0).astype(BF16)
    kf = kf_ref[...].reshape(-1, HEAD_DIM).astype(BF16)
    kr = kr_ref[...].reshape(-1, HEAD_DIM).astype(BF16)
    s_f = lax.dot_general(q, kf, nt, preferred_element_type=F32) * scale + bf_ref[...]
    s_r = lax.dot_general(q, kr, nt, preferred_element_type=F32) * scale + br_ref[...]
    s_n = lax.dot_general(q, kn, nt, preferred_element_type=F32) * scale + bn_ref[...]
    m = jnp.maximum(jnp.maximum(jnp.max(s_f, axis=-1, keepdims=True), jnp.max(s_r, axis=-1, keepdims=True)),
                    jnp.max(s_n, axis=-1, keepdims=True))
    p_f = jnp.exp(s_f - m)
    p_r = jnp.exp(s_r - m)
    p_n = jnp.exp(s_n - m)
    l = (jnp.sum(p_f, axis=-1, keepdims=True) + jnp.sum(p_r, axis=-1, keepdims=True)
         + jnp.sum(p_n, axis=-1, keepdims=True))
    vf = vf_ref[...].reshape(-1, HEAD_DIM).astype(BF16)
    vr = vr_ref[...].reshape(-1, HEAD_DIM).astype(BF16)
    o = (jnp.dot(p_f.astype(BF16), vf, preferred_element_type=F32)
         + jnp.dot(p_r.astype(BF16), vr, preferred_element_type=F32)
         + jnp.dot(p_n.astype(BF16), vn, preferred_element_type=F32)) / l
    for h in range(heads):
        o_ref[:, h * HEAD_DIM:(h + 1) * HEAD_DIM] = o[h * n_new:(h + 1) * n_new]


def _attn_sample(proj, cache_k, cache_v, layer, n_seq, n_heads):
    depth, _, win, _, _ = cache_k.shape
    n_new = proj.shape[0] // n_seq
    hs = SUBLANES
    nhg = n_heads // hs
    (w_far, dil), (w_mid, _) = sorted(DILATED_GROUPS, reverse=True)[:2]
    assert n_new == SUBLANES and win % dil == 0 and (win - w_mid) % dil == 0 and win >= w_far >= w_mid
    n_chunk = win // dil
    far_chunks = (win - w_mid) // dil
    rec_chunks = n_chunk - far_chunks
    assert far_chunks % rec_chunks == 0
    kc = cache_k.reshape(depth * n_seq, n_chunk, dil, n_heads, HEAD_DIM)
    vc = cache_v.reshape(depth * n_seq, n_chunk, dil, n_heads, HEAD_DIM)

    slopes = _alibi_slopes(n_heads).reshape(nhg, hs, 1, 1, 1, 1)
    same_head = (jnp.arange(hs)[:, None, None, None, None] == jnp.arange(hs)[None, None, None, None, :])[None]
    i = jnp.arange(n_new, dtype=jnp.int32)[None, :, None, None, None]

    def table(chunks, rows):
        r = (chunks[:, None] * dil + rows[None, :])[None, None, :, :, None]
        d = jnp.broadcast_to(win + i - r, (1, n_new, len(chunks), len(rows), 1))[None]
        b = jnp.where(same_head, _log_multiplicity(d) - slopes * d.astype(F32), MASKED)
        return b.reshape(nhg, hs * n_new, len(chunks) * len(rows) * hs)

    chunks = jnp.arange(n_chunk, dtype=jnp.int32)
    rows = jnp.arange(dil, dtype=jnp.int32)
    bias_f = table(chunks[:far_chunks], rows[:n_new])
    bias_r = table(chunks[far_chunks:], rows)
    skipped_rows = (np.arange(far_chunks)[:, None] * dil + np.arange(n_new, dil)[None, :]).reshape(-1)
    assert not _multiplicity(win + np.arange(n_new)[:, None] - skipped_rows[None, :]).any()
    j = jnp.arange(LANES, dtype=jnp.int32)
    hj, tj = j // n_new, j % n_new
    d_n = jnp.arange(n_new, dtype=jnp.int32)[None, :, None] - tj[None, None, :]
    ok = (hj[None, None, :] == jnp.arange(hs)[:, None, None]) & (d_n >= 0)
    slopes_n = _alibi_slopes(n_heads).reshape(nhg, hs, 1, 1)
    bias_n = jnp.where(ok[None], _log_multiplicity(jnp.maximum(d_n, 0))[None] - slopes_n * d_n.astype(F32)[None],
                       MASKED).reshape(nhg, hs * n_new, LANES)

    wb = hs * HEAD_DIM
    q_blocks = n_heads * HEAD_DIM // wb
    base = layer * n_seq
    kern = functools.partial(_attn_sample_kernel, heads=hs)
    far_spec = pl.BlockSpec((None, far_chunks, n_new, hs, HEAD_DIM), lambda b, g: (base + b, 0, 0, g, 0))
    rec_spec = pl.BlockSpec((None, rec_chunks, dil, hs, HEAD_DIM),
                            lambda b, g: (base + b, far_chunks // rec_chunks, 0, g, 0))
    return pl.pallas_call(
        kern,
        grid=(n_seq, nhg),
        in_specs=[
            pl.BlockSpec((n_new, wb), lambda b, g: (b, g)),
            pl.BlockSpec((n_new, wb), lambda b, g: (b, q_blocks + g)),
            pl.BlockSpec((n_new, wb), lambda b, g: (b, 2 * q_blocks + g)),
            far_spec, rec_spec, far_spec, rec_spec,
            pl.BlockSpec((None,) + bias_f.shape[1:], lambda b, g: (g, 0, 0)),
            pl.BlockSpec((None,) + bias_r.shape[1:], lambda b, g: (g, 0, 0)),
            pl.BlockSpec((None,) + bias_n.shape[1:], lambda b, g: (g, 0, 0)),
        ],
        out_specs=pl.BlockSpec((n_new, wb), lambda b, g: (b, g)),
        out_shape=jax.ShapeDtypeStruct((n_seq * n_new, n_heads * HEAD_DIM), F32),
        compiler_params=_params("parallel", "arbitrary"),
        name="attn_sample",
    )(proj, proj, proj, kc, kc, vc, vc, bias_f, bias_r, bias_n)


def _s5_pack(a_re, a_im, log_dt, b_re, b_im, c_re, c_im):
    dt = jnp.exp(log_dt.astype(F32))[:, None]
    ar, ai = a_re.astype(F32), a_im.astype(F32)
    mag = jnp.exp(ar * dt)
    lb_re, lb_im = mag * jnp.cos(ai * dt), mag * jnp.sin(ai * dt)
    den = ar * ar + ai * ai
    ir, ii = ar / den, -ai / den
    cr = (lb_re - 1.0) * ir - lb_im * ii
    ci = (lb_re - 1.0) * ii + lb_im * ir
    bb_re = cr[..., None] * b_re - ci[..., None] * b_im
    bb_im = cr[..., None] * b_im + ci[..., None] * b_re
    n_blk = a_re.shape[0] // S5_GROUPS_PER_BLOCK
    gb, p, ch = S5_GROUPS_PER_BLOCK, SSM_STATE, SSM_GROUP_CH
    eye = jnp.eye(gb, dtype=F32)

    def pack_in(bb):
        return jnp.einsum('aGpc,GH->aGcHp', bb.reshape(n_blk, gb, p, ch), eye).reshape(n_blk, gb * ch, gb * p)

    def pack_out(cc):
        return jnp.einsum('aGcp,GH->aHpGc', cc.reshape(n_blk, gb, ch, p), eye).reshape(n_blk, gb * p, gb * ch)

    b_blk = jnp.concatenate([pack_in(bb_re), pack_in(bb_im)], axis=2).astype(BF16)
    c_blk = jnp.concatenate([pack_out(c_re.astype(F32)), -pack_out(c_im.astype(F32))], axis=1).astype(BF16)
    lbr = lb_re.reshape(n_blk, gb * p)
    lbi = lb_im.reshape(n_blk, gb * p)
    return b_blk, c_blk, lbr, lbi


def _s5_prompt_kernel(u_ref, b_ref, c_ref, lbr_ref, lbi_ref, d_ref, z_ref, sre_ref, sim_ref,
                      s_scr, xr_scr, xi_scr):
    i = pl.program_id(1)
    lc = u_ref.shape[0]
    n_blk = b_ref.shape[0]
    n_slab = s_scr.shape[0]
    half = n_slab // 2

    @pl.when(i == 0)
    def _():
        xr_scr[...] = jnp.zeros_like(xr_scr)
        xi_scr[...] = jnp.zeros_like(xi_scr)

    u = u_ref[...]
    ub = u.astype(BF16)
    for g in range(n_blk):
        bu = jnp.dot(ub[:, g * S5_BLOCK_CH:(g + 1) * S5_BLOCK_CH], b_ref[g], preferred_element_type=F32)
        for s in range(n_slab):
            s_scr[s, pl.ds(g, lc, stride=n_blk), :] = bu[:, s * LANES:(s + 1) * LANES]

    def step(t, carry):
        row = pl.multiple_of(t * n_blk, n_blk)
        new = []
        for s in range(half):
            xr, xi = carry[s], carry[half + s]
            lr = lbr_ref[:, s * LANES:(s + 1) * LANES]
            li = lbi_ref[:, s * LANES:(s + 1) * LANES]
            nr = lr * xr - li * xi + s_scr[s, pl.ds(row, n_blk), :]
            ni = lr * xi + li * xr + s_scr[half + s, pl.ds(row, n_blk), :]
            s_scr[s, pl.ds(row, n_blk), :] = nr
            s_scr[half + s, pl.ds(row, n_blk), :] = ni
            new.append((nr, ni))
        return tuple(n[0] for n in new) + tuple(n[1] for n in new)

    init = tuple(xr_scr[:, s * LANES:(s + 1) * LANES] for s in range(half)) + \
        tuple(xi_scr[:, s * LANES:(s + 1) * LANES] for s in range(half))
    fin = lax.fori_loop(0, lc, step, init, unroll=4)
    for s in range(half):
        xr_scr[:, s * LANES:(s + 1) * LANES] = fin[s]
        xi_scr[:, s * LANES:(s + 1) * LANES] = fin[half + s]
    sre_ref[...] = xr_scr[...]
    sim_ref[...] = xi_scr[...]

    for g in range(n_blk):
        xs = jnp.concatenate([s_scr[s, pl.ds(g, lc, stride=n_blk), :] for s in range(n_slab)], axis=1)
        y = jnp.dot(xs.astype(BF16), c_ref[g], preferred_element_type=F32)
        sl = slice(g * S5_BLOCK_CH, (g + 1) * S5_BLOCK_CH)
        y = y + d_ref[:, sl] * u[:, sl]
        z_ref[:, sl] = jax.nn.gelu(y)


def _s5_prompt(proj, n_batch, seq, u_col_block, b_blk, c_blk, lbr, lbi, d_skip, *, lc=128):
    n_blk = b_blk.shape[0]
    assert n_blk == SUBLANES
    d_ssm = n_blk * S5_BLOCK_CH
    nst = S5_BLOCK_STATE
    nc = seq // lc
    n_slab = 2 * nst // LANES
    z, sre, sim = pl.pallas_call(
        _s5_prompt_kernel,
        grid=(n_batch, nc),
        in_specs=[
            pl.BlockSpec((lc, d_ssm), lambda b, i: (b * nc + i, u_col_block)),
            pl.BlockSpec(b_blk.shape, lambda b, i: (0, 0, 0)),
            pl.BlockSpec(c_blk.shape, lambda b, i: (0, 0, 0)),
            pl.BlockSpec(lbr.shape, lambda b, i: (0, 0)),
            pl.BlockSpec(lbi.shape, lambda b, i: (0, 0)),
            pl.BlockSpec((1, d_ssm), lambda b, i: (0, 0)),
        ],
        out_specs=[
            pl.BlockSpec((lc, d_ssm), lambda b, i: (b * nc + i, 0)),
            pl.BlockSpec((None, n_blk, nst), lambda b, i: (b, 0, 0)),
            pl.BlockSpec((None, n_blk, nst), lambda b, i: (b, 0, 0)),
        ],
        out_shape=[
            jax.ShapeDtypeStruct((n_batch * seq, d_ssm), F32),
            jax.ShapeDtypeStruct((n_batch, n_blk, nst), F32),
            jax.ShapeDtypeStruct((n_batch, n_blk, nst), F32),
        ],
        scratch_shapes=[pltpu.VMEM((n_slab, lc * n_blk, LANES), F32),
                        pltpu.VMEM((n_blk, nst), F32), pltpu.VMEM((n_blk, nst), F32)],
        compiler_params=_params("parallel", "arbitrary"),
        name="s5_prompt",
    )(proj, b_blk, c_blk, lbr, lbi, d_skip)
    return z, sre, sim


def _s5_sample_kernel(u_ref, b_ref, c_ref, lbr_ref, lbi_ref, d_ref, x0r_ref, x0i_ref,
                      z_ref, xr_ref, xi_ref):
    n_steps = u_ref.shape[0]
    lr = lbr_ref[...]
    li = lbi_ref[...]
    xr = x0r_ref[...]
    xi = x0i_ref[...]
    nst = xr.shape[1]
    for t in range(n_steps):
        u = u_ref[t]
        bu = jnp.dot(u.astype(BF16), b_ref[...], preferred_element_type=F32)
        xr, xi = lr * xr - li * xi + bu[:, :nst], lr * xi + li * xr + bu[:, nst:]
        xs = jnp.concatenate([xr, xi], axis=1).astype(BF16)
        y = jnp.dot(xs, c_ref[...], preferred_element_type=F32) + d_ref[...] * u
        z_ref[t] = jax.nn.gelu(y)
    xr_ref[...] = xr
    xi_ref[...] = xi


def _s5_sample(u_tm, b_blk, c_blk, lbr, lbi, d_skip, x0r, x0i):
    n_steps, n_seq, d_ssm = u_tm.shape
    n_blk = b_blk.shape[0]
    nst = S5_BLOCK_STATE
    bc = S5_BLOCK_CH
    return pl.pallas_call(
        _s5_sample_kernel,
        grid=(n_blk,),
        in_specs=[
            pl.BlockSpec((n_steps, n_seq, bc), lambda g: (0, 0, g)),
            pl.BlockSpec((None, bc, 2 * nst), lambda g: (g, 0, 0)),
            pl.BlockSpec((None, 2 * nst, bc), lambda g: (g, 0, 0)),
            pl.BlockSpec((None, 1, nst), lambda g: (g, 0, 0)),
            pl.BlockSpec((None, 1, nst), lambda g: (g, 0, 0)),
            pl.BlockSpec((1, bc), lambda g: (0, g)),
            pl.BlockSpec((n_seq, nst), lambda g: (0, g)),
            pl.BlockSpec((n_seq, nst), lambda g: (0, g)),
        ],
        out_specs=[
            pl.BlockSpec((n_steps, n_seq, bc), lambda g: (0, 0, g)),
            pl.BlockSpec((n_seq, nst), lambda g: (0, g)),
            pl.BlockSpec((n_seq, nst), lambda g: (0, g)),
        ],
        out_shape=[
            jax.ShapeDtypeStruct((n_steps, n_seq, d_ssm), F32),
            jax.ShapeDtypeStruct(x0r.shape, F32),
            jax.ShapeDtypeStruct(x0i.shape, F32),
        ],
        compiler_params=_params("parallel"),
        name="s5_sample",
    )(u_tm, b_blk, c_blk, lbr.reshape(n_blk, 1, nst), lbi.reshape(n_blk, 1, nst), d_skip, x0r, x0i)


def _glu_kernel(z_ref, w_ref, b_ref, o_ref):
    z = z_ref[...]
    gate = jnp.dot(z.astype(BF16), w_ref[...], preferred_element_type=F32) + b_ref[...]
    o_ref[...] = (z * jax.nn.sigmoid(gate)).astype(o_ref.dtype)


def _glu(z, w_b, b, *, tm):
    T, N = z.shape
    return pl.pallas_call(
        _glu_kernel,
        grid=(T // tm,),
        in_specs=[
            pl.BlockSpec((tm, N), lambda i: (i, 0)),
            pl.BlockSpec((N, N), lambda i: (0, 0), pipeline_mode=pl.Buffered(1)),
            pl.BlockSpec((1, N), lambda i: (0, 0)),
        ],
        out_specs=pl.BlockSpec((tm, N), lambda i: (i, 0)),
        out_shape=jax.ShapeDtypeStruct((T, N), BF16),
        compiler_params=_params("parallel"),
        name="glu",
    )(z, w_b, b)


def _out_proj_kernel(a1_ref, a2_ref, w1_ref, w2_ref, x_ref, o_ref):
    acc = jnp.dot(a1_ref[...], w1_ref[...], preferred_element_type=F32)
    acc = acc + jnp.dot(a2_ref[...], w2_ref[...], preferred_element_type=F32)
    o_ref[...] = x_ref[...] + acc


def _out_proj(o_att, o_ssm, w_b, x2d, *, tm, tn):
    T, D = x2d.shape
    k1, k2 = o_att.shape[1], o_ssm.shape[1]
    assert k1 == k2
    return pl.pallas_call(
        _out_proj_kernel,
        grid=(T // tm, D // tn),
        in_specs=[
            pl.BlockSpec((tm, k1), lambda i, j: (i, 0)),
            pl.BlockSpec((tm, k2), lambda i, j: (i, 0)),
            pl.BlockSpec((k1, tn), lambda i, j: (0, j)),
            pl.BlockSpec((k2, tn), lambda i, j: (1, j)),
            pl.BlockSpec((tm, tn), lambda i, j: (i, j)),
        ],
        out_specs=pl.BlockSpec((tm, tn), lambda i, j: (i, j)),
        out_shape=jax.ShapeDtypeStruct((T, D), F32),
        compiler_params=_params("parallel", "arbitrary"),
        name="out_proj",
    )(o_att, o_ssm, w_b, w_b, x2d)


def _ffn_kernel(*refs, seq_len, has_state, n_chunk):
    if has_state:
        (x_ref, halo_ref, g_ref, wg_ref, wu_ref, wd_ref, cw_ref, cb_ref, e1_ref, e2_ref,
         y_ref, a_ref, h_scr) = refs
    else:
        (x_ref, halo_ref, g_ref, wg_ref, wu_ref, wd_ref, cw_ref, cb_ref,
         y_ref, a_ref, h_scr) = refs
    i = pl.program_id(0)
    f = pl.program_id(1)
    tm = x_ref.shape[0]
    tf = wg_ref.shape[1]

    @pl.when(f == 0)
    def _():
        _rms_rows_to(h_scr, 0, halo_ref, g_ref)
        _rms_rows_to(h_scr, FFN_HALO, x_ref, g_ref)
        y_ref[...] = x_ref[...]

    a_ext = jnp.dot(h_scr[...], wg_ref[...], preferred_element_type=F32)
    a = a_ext[FFN_HALO:]
    pos = (i * tm + lax.broadcasted_iota(jnp.int32, (tm, tf), 0)) % seq_len
    prev1 = a_ext[FFN_HALO - 1:FFN_HALO - 1 + tm]
    prev2 = a_ext[FFN_HALO - 2:FFN_HALO - 2 + tm]
    if has_state:
        prev1 = jnp.where(pos >= 1, prev1, e1_ref[...])
        prev2 = jnp.where(pos >= 2, prev2, e2_ref[...])
    else:
        prev1 = jnp.where(pos >= 1, prev1, 0.0)
        prev2 = jnp.where(pos >= 2, prev2, 0.0)
    conv = cb_ref[...] + cw_ref[0:1, :] * prev2 + cw_ref[1:2, :] * prev1 + cw_ref[2:3, :] * a
    up = jnp.dot(h_scr[FFN_HALO:, :], wu_ref[...], preferred_element_type=F32)
    gated = (jax.nn.silu(conv) * up).astype(BF16)
    d_out = y_ref.shape[1]
    for n in range(d_out // n_chunk):
        sl = slice(n * n_chunk, (n + 1) * n_chunk)
        y_ref[:, sl] += jnp.dot(gated, wd_ref[:, sl], preferred_element_type=F32)
    if has_state:
        a_ref[...] = a
    else:
        a_ref[...] = a[tm - SUBLANES:]


def _ffn(x1, g2, wg_b, wu_b, wd_b, conv_w, conv_b, seq_len, e1=None, e2=None, *, tm, tf):
    T, D = x1.shape
    F = wg_b.shape[1]
    has_state = e1 is not None
    assert (seq_len % tm == 0 or tm % seq_len == 0) and F % tf == 0
    blocks_per_halo = tm // FFN_HALO
    in_specs = [
        pl.BlockSpec((tm, D), lambda i, f: (i, 0), pipeline_mode=pl.Buffered(1)),
        pl.BlockSpec((FFN_HALO, D), lambda i, f: (jnp.maximum(i * blocks_per_halo - 1, 0), 0)),
        pl.BlockSpec((1, D), lambda i, f: (0, 0)),
        pl.BlockSpec((D, tf), lambda i, f: (0, f)),
        pl.BlockSpec((D, tf), lambda i, f: (0, f)),
        pl.BlockSpec((tf, D), lambda i, f: (f, 0)),
        pl.BlockSpec((CONV_W, tf), lambda i, f: (0, f)),
        pl.BlockSpec((1, tf), lambda i, f: (0, f)),
    ]
    args = [x1, x1, g2, wg_b, wu_b, wd_b, conv_w, conv_b]
    if has_state:
        in_specs += [pl.BlockSpec((tm, tf), lambda i, f: (i, f))] * 2
        args += [e1, e2]
        a_spec = pl.BlockSpec((tm, tf), lambda i, f: (i, f))
        a_shape = jax.ShapeDtypeStruct((T, F), F32)
    else:
        a_spec = pl.BlockSpec((SUBLANES, tf), lambda i, f: (i, f))
        a_shape = jax.ShapeDtypeStruct((T // tm * SUBLANES, F), F32)
    kern = functools.partial(_ffn_kernel, seq_len=seq_len, has_state=has_state, n_chunk=min(D, 512))
    return pl.pallas_call(
        kern,
        grid=(T // tm, F // tf),
        in_specs=in_specs,
        out_specs=[pl.BlockSpec((tm, D), lambda i, f: (i, 0), pipeline_mode=pl.Buffered(1)), a_spec],
        out_shape=[jax.ShapeDtypeStruct((T, D), F32), a_shape],
        scratch_shapes=[pltpu.VMEM((FFN_HALO + tm, D), BF16)],
        compiler_params=_params("parallel", "arbitrary"),
        name="ffn",
    )(*args)


def _layer(x3d, lw, *, layer=0, cache=None, ssm0=None, conv0=None):
    (norm1_g, w_in, q_norm_g, k_norm_g, a_re, a_im, log_dt, b_re, b_im, c_re, c_im, ssm_d,
     w_glu, b_glu, w_out, norm2_g, w_gate, w_up, conv_w, conv_b, w_down) = lw
    nb, L, D = x3d.shape
    T = nb * L
    x2d = x3d.reshape(T, D)
    d_ssm = ssm_d.shape[0]
    d_att = (w_in.shape[1] - d_ssm) // 3
    n_heads = d_att // HEAD_DIM
    n_groups = a_re.shape[0]
    tm = min(PROJ_TM, T)
    tm_ffn = min(FFN_TM, T)

    head_gain = jnp.concatenate([jnp.tile(q_norm_g.astype(F32), n_heads),
                                 jnp.tile(k_norm_g.astype(F32), n_heads)])[None]
    proj = _in_proj(x2d, norm1_g[None], w_in.astype(BF16), head_gain, tm=tm, tn=1024)
    k = proj[:, d_att:2 * d_att].reshape(nb, L, n_heads, HEAD_DIM)
    v = proj[:, 2 * d_att:3 * d_att].reshape(nb, L, n_heads, HEAD_DIM)

    b_blk, c_blk, lbr, lbi = _s5_pack(a_re, a_im, log_dt, b_re, b_im, c_re, c_im)
    d_skip = ssm_d.astype(F32)[None]
    if cache is None:
        o_att = _attn_prompt(proj, nb, L, n_heads)
        z, sre, sim = _s5_prompt(proj, nb, L, 3 * d_att // d_ssm, b_blk, c_blk, lbr, lbi, d_skip)
    else:
        o_att = _attn_sample(proj, cache[0], cache[1], layer, nb, n_heads).astype(BF16)
        u_tm = proj[:, 3 * d_att:].reshape(nb, L, d_ssm).transpose(1, 0, 2)
        z_tm, sre, sim = _s5_sample(u_tm, b_blk, c_blk, lbr, lbi, d_skip,
                                    ssm0[0].astype(F32).reshape(nb, n_groups * SSM_STATE),
                                    ssm0[1].astype(F32).reshape(nb, n_groups * SSM_STATE))
        z = z_tm.transpose(1, 0, 2).reshape(T, d_ssm)
    ssm_re = sre.reshape(nb, n_groups, SSM_STATE)
    ssm_im = sim.reshape(nb, n_groups, SSM_STATE)

    o_ssm = _glu(z, w_glu.astype(BF16), b_glu.astype(F32)[None], tm=tm)
    x1 = _out_proj(o_att, o_ssm, w_out.astype(BF16), x2d, tm=tm, tn=1024)

    F = w_gate.shape[1]
    fpad = -F % FFN_TF
    pad_cols = lambda w: jnp.pad(w, ((0, 0), (0, fpad)))
    ffn_w = (norm2_g[None], pad_cols(w_gate).astype(BF16), pad_cols(w_up).astype(BF16),
             jnp.pad(w_down, ((0, fpad), (0, 0))).astype(BF16),
             pad_cols(conv_w.astype(F32)), pad_cols(conv_b.astype(F32)[None]))
    if conv0 is None:
        y, a_tail = _ffn(x1, *ffn_w, L, tm=tm_ffn, tf=FFN_TF)
        tiles_per_seq = L // tm_ffn
        a_tail = a_tail.reshape(nb, tiles_per_seq, SUBLANES, F + fpad)
        conv_state = a_tail[:, -1, SUBLANES - (CONV_W - 1):, :F]
    else:
        c0 = jnp.pad(conv0.astype(F32), ((0, 0), (0, 0), (0, fpad)))
        zeros = jnp.zeros((nb, L - 1, F + fpad), F32)
        e1 = jnp.concatenate([c0[:, 1:2], zeros], axis=1).reshape(T, F + fpad)
        e2 = jnp.concatenate([c0[:, 0:1], c0[:, 1:2], zeros[:, 1:]], axis=1).reshape(T, F + fpad)
        y, a_full = _ffn(x1, *ffn_w, L, e1, e2, tm=tm_ffn, tf=FFN_TF)
        conv_state = a_full.reshape(nb, L, F + fpad)[:, L - (CONV_W - 1):, :F]
    return y.reshape(nb, L, D), k, v, ssm_re, ssm_im, conv_state


def kernel(x_prompt, x_sample, cache_k, cache_v, state_ssm_re, state_ssm_im, state_ffn_conv,
           norm1_g, w_in, q_norm_g, k_norm_g,
           ssm_a_re, ssm_a_im, ssm_log_dt, ssm_b_re, ssm_b_im, ssm_c_re, ssm_c_im,
           ssm_d, w_glu, b_glu, w_out, norm2_g,
           w_ffn_gate, w_ffn_up, ffn_conv_w, ffn_conv_b, w_ffn_down):
    depth = w_in.shape[0]
    seq = x_prompt.shape[1]
    wb_prompt = min(MAX_WINDOW, seq)
    outs_p = [[] for _ in range(5)]
    outs_s = [[] for _ in range(5)]
    yp, ys = x_prompt, x_sample
    for l in range(depth):
        lw = (norm1_g[l], w_in[l], q_norm_g[l], k_norm_g[l],
              ssm_a_re[l], ssm_a_im[l], ssm_log_dt[l], ssm_b_re[l], ssm_b_im[l], ssm_c_re[l], ssm_c_im[l],
              ssm_d[l], w_glu[l], b_glu[l], w_out[l], norm2_g[l],
              w_ffn_gate[l], w_ffn_up[l], ffn_conv_w[l], ffn_conv_b[l], w_ffn_down[l])
        yp, k_p, v_p, re_p, im_p, conv_p = _layer(yp, lw)
        ys, k_s, v_s, re_s, im_s, conv_s = _layer(
            ys, lw, layer=l, cache=(cache_k, cache_v),
            ssm0=(state_ssm_re[l], state_ssm_im[l]), conv0=state_ffn_conv[l])
        for lst, val in zip(outs_p, (k_p[:, seq - wb_prompt:], v_p[:, seq - wb_prompt:], re_p, im_p, conv_p)):
            lst.append(val)
        for lst, val in zip(outs_s, (k_s, v_s, re_s, im_s, conv_s)):
            lst.append(val)
    return (yp, ys, *(jnp.stack(o) for o in outs_p), *(jnp.stack(o) for o in outs_s))
```

```python
import functools
import math

import jax
import jax.numpy as jnp
import numpy as np
from jax import lax
from jax.experimental import pallas as pl
from jax.experimental.pallas import tpu as pltpu

F32 = jnp.float32
BF16 = jnp.bfloat16

HEAD_DIM = 128
DILATED_GROUPS = ((128, 1), (512, 4), (2048, 16))
MAX_WINDOW = max(w for w, _ in DILATED_GROUPS)
SSM_GROUP_CH = 16
SSM_STATE = 64
CONV_W = 3
NORM_EPS = 1e-6
MASKED = -1e30

LANES = 128
SUBLANES = 8
MXU_DIM = 256
VMEM_LIMIT_BYTES = 60 * 1024 * 1024

S5_GROUPS_PER_BLOCK = MXU_DIM // SSM_GROUP_CH
S5_BLOCK_CH = S5_GROUPS_PER_BLOCK * SSM_GROUP_CH
S5_BLOCK_STATE = S5_GROUPS_PER_BLOCK * SSM_STATE

ATTN_TQ = 256
ATTN_QBLOCKS = 2
S5_CHUNK = 256
PROJ_TM = 1024
IN_PROJ_TM = 512
FFN_TM = 512
FFN_STATE_TM = 256
FFN_TF = 512
FFN_HALO = 16


def _params(*sem):
    return pltpu.CompilerParams(dimension_semantics=sem, vmem_limit_bytes=VMEM_LIMIT_BYTES)


def _rms_rows(x, g):
    ms = jnp.mean(x * x, axis=-1, keepdims=True)
    return x * lax.rsqrt(ms + NORM_EPS) * g


def _rms_rows_to(dst_ref, dst_row0, src_ref, g_ref, chunk=128):
    rows = src_ref.shape[0]
    chunk = min(chunk, rows)

    def body(c, carry):
        r0 = pl.multiple_of(c * chunk, chunk)
        dst_ref[pl.ds(dst_row0 + r0, chunk), :] = _rms_rows(src_ref[pl.ds(r0, chunk), :], g_ref[...]).astype(BF16)
        return carry

    lax.fori_loop(0, rows // chunk, body, 0)


def _in_proj_kernel(x_ref, g_ref, w_ref, hg_ref, o_ref, ko_ref, vo_ref, h_scr, *, blocks_per_part):
    j = pl.program_id(1)
    nb = blocks_per_part
    heads = o_ref.shape[1] // HEAD_DIM

    @pl.when(j == 0)
    def _():
        _rms_rows_to(h_scr, 0, x_ref, g_ref)

    acc = jnp.dot(h_scr[...], w_ref[...], preferred_element_type=F32)

    def head_normed(hh):
        sl = slice(hh * HEAD_DIM, (hh + 1) * HEAD_DIM)
        return sl, _rms_rows(acc[:, sl], hg_ref[:, sl])

    @pl.when(j < nb)
    def _():
        for hh in range(heads):
            sl, val = head_normed(hh)
            o_ref[:, sl] = val

    @pl.when(jnp.logical_and(j >= nb, j < 2 * nb))
    def _():
        for hh in range(heads):
            sl, val = head_normed(hh)
            o_ref[:, sl] = val
            ko_ref[:, hh, :] = val

    @pl.when(jnp.logical_and(j >= 2 * nb, j < 3 * nb))
    def _():
        o_ref[...] = acc
        for hh in range(heads):
            vo_ref[:, hh, :] = acc[:, hh * HEAD_DIM:(hh + 1) * HEAD_DIM]

    @pl.when(j >= 3 * nb)
    def _():
        o_ref[...] = acc


def _in_proj(x2d, g1, w_b, head_gain, n_heads, *, tm, tn):
    T, D = x2d.shape
    N = w_b.shape[1]
    assert tn == SUBLANES * HEAD_DIM and (n_heads * HEAD_DIM) % tn == 0
    nb = n_heads * HEAD_DIM // tn
    kern = functools.partial(_in_proj_kernel, blocks_per_part=nb)
    kv_shape = jax.ShapeDtypeStruct((T, n_heads, HEAD_DIM), F32)
    return pl.pallas_call(
        kern,
        grid=(T // tm, N // tn),
        in_specs=[
            pl.BlockSpec((tm, D), lambda i, j: (i, 0), pipeline_mode=pl.Buffered(1)),
            pl.BlockSpec((1, D), lambda i, j: (0, 0)),
            pl.BlockSpec((D, tn), lambda i, j: (0, j)),
            pl.BlockSpec((1, tn), lambda i, j: (0, jnp.minimum(j, 2 * nb - 1))),
        ],
        out_specs=[
            pl.BlockSpec((tm, tn), lambda i, j: (i, j)),
            pl.BlockSpec((tm, SUBLANES, HEAD_DIM), lambda i, j: (i, jnp.clip(j - nb, 0, nb - 1), 0)),
            pl.BlockSpec((tm, SUBLANES, HEAD_DIM), lambda i, j: (i, jnp.clip(j - 2 * nb, 0, nb - 1), 0)),
        ],
        out_shape=[jax.ShapeDtypeStruct((T, N), F32), kv_shape, kv_shape],
        scratch_shapes=[pltpu.VMEM((tm, D), BF16)],
        compiler_params=_params("parallel", "arbitrary"),
        name="in_proj",
    )(x2d, g1, w_b, head_gain)


LOG2E = math.log2(math.e)


def _multiplicity(d):
    count = np.zeros(d.shape, np.int32)
    for window, dilation in DILATED_GROUPS:
        count += ((d >= 0) & (d <= window) & (d % dilation == 0)).astype(np.int32)
    return count


def _bias2(d, slope=0.0):
    count = _multiplicity(d)
    return np.where(count > 0, np.log2(np.maximum(count, 1)) - slope * LOG2E * d, MASKED).astype(np.float32)


def _alibi_slopes(n_heads):
    return (2.0 ** (-8.0 * np.arange(1, n_heads + 1, dtype=np.float64) / n_heads)).astype(np.float32)


def _attn_prompt_kernel(slopes_ref, q_ref, k_ref, v_ref, logc_ref, o_ref, kb_scr, vb_scr, bias_scr,
                        *, n_off):
    h = pl.program_id(1)
    qi = pl.program_id(2)
    tq = bias_scr.shape[0]
    pad = (n_off - 1) * tq

    @pl.when(qi == 0)
    def _():
        kb_scr[0:pad, :] = jnp.zeros((pad, HEAD_DIM), BF16)
        vb_scr[0:pad, :] = jnp.zeros((pad, HEAD_DIM), BF16)
        kb_scr[pad:, :] = k_ref[...].astype(BF16)
        vb_scr[pad:, :] = v_ref[...].astype(BF16)
        slope = slopes_ref[h]
        row = lax.broadcasted_iota(jnp.int32, (tq, tq), 0)
        col = lax.broadcasted_iota(jnp.int32, (tq, tq), 1)
        base = (row - col).astype(F32)
        for j in range(n_off):
            m = n_off - 1 - j
            bias_scr[:, j * tq:(j + 1) * tq] = logc_ref[m] - (slope * LOG2E) * (base + float(m * tq))

    n_sub = q_ref.shape[0] // tq

    def blocks(masked):
        for sub in range(n_sub):
            _attn_prompt_block(qi * n_sub + sub, q_ref.at[sub * tq:(sub + 1) * tq],
                               o_ref.at[sub * tq:(sub + 1) * tq], kb_scr, vb_scr, bias_scr,
                               n_off=n_off, masked=masked)

    pl.when(qi * n_sub >= n_off - 1)(lambda: blocks(False))
    pl.when(qi * n_sub < n_off - 1)(lambda: blocks(True))


def _attn_prompt_block(qb, q_ref, o_ref, kb_scr, vb_scr, bias_scr, *, n_off, masked):
    tq = q_ref.shape[0]
    wlen = n_off * tq
    q = (q_ref[...] * (HEAD_DIM ** -0.5 * LOG2E)).astype(BF16)
    start = pl.multiple_of(qb * tq, tq)
    kw = kb_scr[pl.ds(start, wlen), :]
    vw = vb_scr[pl.ds(start, wlen), :]
    s = lax.dot_general(q, kw, (((1,), (1,)), ((), ())), preferred_element_type=F32) + bias_scr[...]
    if masked:
        lane = lax.broadcasted_iota(jnp.int32, (1, wlen), 1)
        s = s + jnp.where(lane < (n_off - 1 - qb) * tq, MASKED, 0.0)
    m_i = jnp.max(s, axis=-1, keepdims=True)
    p = jnp.exp2(s - m_i)
    l_i = jnp.sum(p, axis=-1, keepdims=True)
    acc = jnp.dot(p.astype(BF16), vw, preferred_element_type=F32)
    o_ref[...] = (acc / l_i).astype(o_ref.dtype)


def _attn_prompt(proj, n_batch, seq, n_heads):
    tq = ATTN_TQ
    rows = ATTN_QBLOCKS * tq
    nq = seq // rows
    n_off = MAX_WINDOW // tq + 1
    m = np.arange(n_off)[:, None, None]
    r = np.arange(tq)[None, :, None]
    c = np.arange(tq)[None, None, :]
    logc = _bias2(m * tq + r - c)
    kern = functools.partial(_attn_prompt_kernel, n_off=n_off)
    pad = (n_off - 1) * tq
    return pl.pallas_call(
        kern,
        grid=(n_batch, n_heads, nq),
        in_specs=[
            pl.BlockSpec(memory_space=pltpu.SMEM),
            pl.BlockSpec((rows, HEAD_DIM), lambda b, h, i: (b * nq + i, h)),
            pl.BlockSpec((seq, HEAD_DIM), lambda b, h, i: (b, n_heads + h)),
            pl.BlockSpec((seq, HEAD_DIM), lambda b, h, i: (b, 2 * n_heads + h)),
            pl.BlockSpec((n_off, tq, tq), lambda b, h, i: (0, 0, 0)),
        ],
        out_specs=pl.BlockSpec((rows, HEAD_DIM), lambda b, h, i: (b * nq + i, h)),
        out_shape=jax.ShapeDtypeStruct((n_batch * seq, n_heads * HEAD_DIM), BF16),
        scratch_shapes=[pltpu.VMEM((pad + seq, HEAD_DIM), BF16), pltpu.VMEM((pad + seq, HEAD_DIM), BF16),
                        pltpu.VMEM((tq, n_off * tq), F32)],
        compiler_params=_params("parallel", "parallel", "arbitrary"),
        name="attn_prompt",
    )(_alibi_slopes(n_heads), proj, proj, proj, logc)


def _attn_sample_kernel(q_ref, kn_ref, vn_ref, kf_ref, kr_ref, vf_ref, vr_ref,
                        bf_ref, br_ref, bn_ref, o_ref, *, heads):
    n_new = q_ref.shape[0]
    nt = (((1,), (1,)), ((), ()))

    def by_head(ref):
        return jnp.concatenate([ref[:, h * HEAD_DIM:(h + 1) * HEAD_DIM] for h in range(heads)], axis=0)

    q = (by_head(q_ref) * (HEAD_DIM ** -0.5 * LOG2E)).astype(BF16)
    pad = jnp.zeros((LANES - heads * n_new, HEAD_DIM), F32)
    kn = jnp.concatenate([by_head(kn_ref), pad], axis=0).astype(BF16)
    vn = jnp.concatenate([by_head(vn_ref), pad], axis=0).astype(BF16)
    kf = kf_ref[...].reshape(-1, HEAD_DIM).astype(BF16)
    kr = kr_ref[...].reshape(-1, HEAD_DIM).astype(BF16)
    s_f = lax.dot_general(q, kf, nt, preferred_element_type=F32) + bf_ref[...]
    s_r = lax.dot_general(q, kr, nt, preferred_element_type=F32) + br_ref[...]
    s_n = lax.dot_general(q, kn, nt, preferred_element_type=F32) + bn_ref[...]
    m = jnp.maximum(jnp.maximum(jnp.max(s_f, axis=-1, keepdims=True), jnp.max(s_r, axis=-1, keepdims=True)),
                    jnp.max(s_n, axis=-1, keepdims=True))
    p_f = jnp.exp2(s_f - m)
    p_r = jnp.exp2(s_r - m)
    p_n = jnp.exp2(s_n - m)
    l = (jnp.sum(p_f, axis=-1, keepdims=True) + jnp.sum(p_r, axis=-1, keepdims=True)
         + jnp.sum(p_n, axis=-1, keepdims=True))
    vf = vf_ref[...].reshape(-1, HEAD_DIM).astype(BF16)
    vr = vr_ref[...].reshape(-1, HEAD_DIM).astype(BF16)
    o = (jnp.dot(p_f.astype(BF16), vf, preferred_element_type=F32)
         + jnp.dot(p_r.astype(BF16), vr, preferred_element_type=F32)
         + jnp.dot(p_n.astype(BF16), vn, preferred_element_type=F32)) / l
    for h in range(heads):
        o_ref[:, h * HEAD_DIM:(h + 1) * HEAD_DIM] = o[h * n_new:(h + 1) * n_new]


def _attn_sample(proj, cache_k, cache_v, layer, n_seq, n_heads):
    depth, _, win, _, _ = cache_k.shape
    n_new = proj.shape[0] // n_seq
    hs = SUBLANES
    nhg = n_heads // hs
    (w_far, dil), (w_mid, _) = sorted(DILATED_GROUPS, reverse=True)[:2]
    assert n_new == SUBLANES and win % dil == 0 and (win - w_mid) % dil == 0 and win >= w_far >= w_mid
    n_chunk = win // dil
    far_chunks = (win - w_mid) // dil
    rec_chunks = n_chunk - far_chunks
    assert far_chunks % rec_chunks == 0
    kc = cache_k.reshape(depth * n_seq, n_chunk, dil, n_heads, HEAD_DIM)
    vc = cache_v.reshape(depth * n_seq, n_chunk, dil, n_heads, HEAD_DIM)

    slopes = _alibi_slopes(n_heads).astype(np.float64)
    same_head = np.arange(hs)[:, None, None, None, None] == np.arange(hs)[None, None, None, None, :]
    i = np.arange(n_new)[None, :, None, None, None]

    def table(chunks, rows):
        r = (chunks[:, None] * dil + rows[None, :])[None, None, :, :, None]
        d = np.broadcast_to(win + i - r, (hs, n_new, len(chunks), len(rows), hs))
        out = [np.where(same_head, _bias2(d, slopes[g * hs:(g + 1) * hs, None, None, None, None]), MASKED)
               for g in range(nhg)]
        return np.stack(out).reshape(nhg, hs * n_new, -1).astype(np.float32)

    chunks = np.arange(n_chunk)
    rows = np.arange(dil)
    bias_f = table(chunks[:far_chunks], rows[:n_new])
    bias_r = table(chunks[far_chunks:], rows)
    skipped_rows = (chunks[:far_chunks, None] * dil + rows[None, n_new:]).reshape(-1)
    assert not _multiplicity(win + np.arange(n_new)[:, None] - skipped_rows[None, :]).any()
    j = np.arange(LANES)
    hj, tj = j // n_new, j % n_new
    d_n = np.broadcast_to(np.arange(n_new)[None, :, None] - tj[None, None, :], (hs, n_new, LANES))
    own = hj[None, None, :] == np.arange(hs)[:, None, None]
    bias_n = np.stack([np.where(own, _bias2(d_n, slopes[g * hs:(g + 1) * hs, None, None]), MASKED)
                       for g in range(nhg)]).reshape(nhg, hs * n_new, LANES).astype(np.float32)

    wb = hs * HEAD_DIM
    q_blocks = n_heads * HEAD_DIM // wb
    base = layer * n_seq
    kern = functools.partial(_attn_sample_kernel, heads=hs)
    far_spec = pl.BlockSpec((None, far_chunks, n_new, hs, HEAD_DIM), lambda b, g: (base + b, 0, 0, g, 0))
    rec_spec = pl.BlockSpec((None, rec_chunks, dil, hs, HEAD_DIM),
                            lambda b, g: (base + b, far_chunks // rec_chunks, 0, g, 0))
    return pl.pallas_call(
        kern,
        grid=(n_seq, nhg),
        in_specs=[
            pl.BlockSpec((n_new, wb), lambda b, g: (b, g)),
            pl.BlockSpec((n_new, wb), lambda b, g: (b, q_blocks + g)),
            pl.BlockSpec((n_new, wb), lambda b, g: (b, 2 * q_blocks + g)),
            far_spec, rec_spec, far_spec, rec_spec,
            pl.BlockSpec((None,) + bias_f.shape[1:], lambda b, g: (g, 0, 0)),
            pl.BlockSpec((None,) + bias_r.shape[1:], lambda b, g: (g, 0, 0)),
            pl.BlockSpec((None,) + bias_n.shape[1:], lambda b, g: (g, 0, 0)),
        ],
        out_specs=pl.BlockSpec((n_new, wb), lambda b, g: (b, g)),
        out_shape=jax.ShapeDtypeStruct((n_seq * n_new, n_heads * HEAD_DIM), F32),
        compiler_params=_params("parallel", "arbitrary"),
        name="attn_sample",
    )(proj, proj, proj, kc, kc, vc, vc, bias_f, bias_r, bias_n)


def _s5_pack(a_re, a_im, log_dt, b_re, b_im, c_re, c_im):
    dt = jnp.exp(log_dt.astype(F32))[:, None]
    ar, ai = a_re.astype(F32), a_im.astype(F32)
    mag = jnp.exp(ar * dt)
    lb_re, lb_im = mag * jnp.cos(ai * dt), mag * jnp.sin(ai * dt)
    den = ar * ar + ai * ai
    ir, ii = ar / den, -ai / den
    cr = (lb_re - 1.0) * ir - lb_im * ii
    ci = (lb_re - 1.0) * ii + lb_im * ir
    bb_re = cr[..., None] * b_re - ci[..., None] * b_im
    bb_im = cr[..., None] * b_im + ci[..., None] * b_re
    n_blk = a_re.shape[0] // S5_GROUPS_PER_BLOCK
    gb, p, ch = S5_GROUPS_PER_BLOCK, SSM_STATE, SSM_GROUP_CH
    eye = jnp.eye(gb, dtype=F32)

    def pack_in(bb):
        return jnp.einsum('aGpc,GH->aGcHp', bb.reshape(n_blk, gb, p, ch), eye).reshape(n_blk, gb * ch, gb * p)

    def pack_out(cc):
        return jnp.einsum('aGcp,GH->aHpGc', cc.reshape(n_blk, gb, ch, p), eye).reshape(n_blk, gb * p, gb * ch)

    b_blk = jnp.concatenate([pack_in(bb_re), pack_in(bb_im)], axis=2).astype(BF16)
    c_blk = jnp.concatenate([pack_out(c_re.astype(F32)), -pack_out(c_im.astype(F32))], axis=1).astype(BF16)
    lbr = lb_re.reshape(n_blk, gb * p)
    lbi = lb_im.reshape(n_blk, gb * p)
    return b_blk, c_blk, lbr, lbi


def _s5_prompt_kernel(u_ref, b_ref, c_ref, lbr_ref, lbi_ref, d_ref, z_ref, sre_ref, sim_ref,
                      s_scr, xr_scr, xi_scr):
    i = pl.program_id(1)
    lc = u_ref.shape[0]
    n_blk = b_ref.shape[0]
    n_slab = s_scr.shape[0]
    half = n_slab // 2

    @pl.when(i == 0)
    def _():
        xr_scr[...] = jnp.zeros_like(xr_scr)
        xi_scr[...] = jnp.zeros_like(xi_scr)

    u = u_ref[...]
    ub = u.astype(BF16)
    for g in range(n_blk):
        bu = jnp.dot(ub[:, g * S5_BLOCK_CH:(g + 1) * S5_BLOCK_CH], b_ref[g], preferred_element_type=F32)
        for s in range(n_slab):
            s_scr[s, pl.ds(g, lc, stride=n_blk), :] = bu[:, s * LANES:(s + 1) * LANES]

    def step(t, carry):
        row = pl.multiple_of(t * n_blk, n_blk)
        new = []
        for s in range(half):
            xr, xi = carry[s], carry[half + s]
            lr = lbr_ref[:, s * LANES:(s + 1) * LANES]
            li = lbi_ref[:, s * LANES:(s + 1) * LANES]
            nr = lr * xr - li * xi + s_scr[s, pl.ds(row, n_blk), :]
            ni = lr * xi + li * xr + s_scr[half + s, pl.ds(row, n_blk), :]
            s_scr[s, pl.ds(row, n_blk), :] = nr
            s_scr[half + s, pl.ds(row, n_blk), :] = ni
            new.append((nr, ni))
        return tuple(n[0] for n in new) + tuple(n[1] for n in new)

    init = tuple(xr_scr[:, s * LANES:(s + 1) * LANES] for s in range(half)) + \
        tuple(xi_scr[:, s * LANES:(s + 1) * LANES] for s in range(half))
    fin = lax.fori_loop(0, lc, step, init, unroll=4)
    for s in range(half):
        xr_scr[:, s * LANES:(s + 1) * LANES] = fin[s]
        xi_scr[:, s * LANES:(s + 1) * LANES] = fin[half + s]
    sre_ref[...] = xr_scr[...]
    sim_ref[...] = xi_scr[...]

    for g in range(n_blk):
        xs = jnp.concatenate([s_scr[s, pl.ds(g, lc, stride=n_blk), :] for s in range(n_slab)], axis=1)
        y = jnp.dot(xs.astype(BF16), c_ref[g], preferred_element_type=F32)
        sl = slice(g * S5_BLOCK_CH, (g + 1) * S5_BLOCK_CH)
        y = y + d_ref[:, sl] * u[:, sl]
        z_ref[:, sl] = jax.nn.gelu(y)


def _s5_prompt(proj, n_batch, seq, u_col_block, b_blk, c_blk, lbr, lbi, d_skip, *, lc=S5_CHUNK):
    n_blk = b_blk.shape[0]
    assert n_blk == SUBLANES
    d_ssm = n_blk * S5_BLOCK_CH
    nst = S5_BLOCK_STATE
    nc = seq // lc
    n_slab = 2 * nst // LANES
    z, sre, sim = pl.pallas_call(
        _s5_prompt_kernel,
        grid=(n_batch, nc),
        in_specs=[
            pl.BlockSpec((lc, d_ssm), lambda b, i: (b * nc + i, u_col_block)),
            pl.BlockSpec(b_blk.shape, lambda b, i: (0, 0, 0), pipeline_mode=pl.Buffered(1)),
            pl.BlockSpec(c_blk.shape, lambda b, i: (0, 0, 0), pipeline_mode=pl.Buffered(1)),
            pl.BlockSpec(lbr.shape, lambda b, i: (0, 0)),
            pl.BlockSpec(lbi.shape, lambda b, i: (0, 0)),
            pl.BlockSpec((1, d_ssm), lambda b, i: (0, 0)),
        ],
        out_specs=[
            pl.BlockSpec((lc, d_ssm), lambda b, i: (b * nc + i, 0)),
            pl.BlockSpec((None, n_blk, nst), lambda b, i: (b, 0, 0)),
            pl.BlockSpec((None, n_blk, nst), lambda b, i: (b, 0, 0)),
        ],
        out_shape=[
            jax.ShapeDtypeStruct((n_batch * seq, d_ssm), F32),
            jax.ShapeDtypeStruct((n_batch, n_blk, nst), F32),
            jax.ShapeDtypeStruct((n_batch, n_blk, nst), F32),
        ],
        scratch_shapes=[pltpu.VMEM((n_slab, lc * n_blk, LANES), F32),
                        pltpu.VMEM((n_blk, nst), F32), pltpu.VMEM((n_blk, nst), F32)],
        compiler_params=_params("parallel", "arbitrary"),
        name="s5_prompt",
    )(proj, b_blk, c_blk, lbr, lbi, d_skip)
    return z, sre, sim


def _s5_sample_kernel(u_ref, b_ref, c_ref, lbr_ref, lbi_ref, d_ref, x0r_ref, x0i_ref,
                      z_ref, xr_ref, xi_ref):
    n_steps = u_ref.shape[0]
    lr = lbr_ref[...]
    li = lbi_ref[...]
    xr = x0r_ref[...]
    xi = x0i_ref[...]
    nst = xr.shape[1]
    for t in range(n_steps):
        u = u_ref[t]
        bu = jnp.dot(u.astype(BF16), b_ref[...], preferred_element_type=F32)
        xr, xi = lr * xr - li * xi + bu[:, :nst], lr * xi + li * xr + bu[:, nst:]
        xs = jnp.concatenate([xr, xi], axis=1).astype(BF16)
        y = jnp.dot(xs, c_ref[...], preferred_element_type=F32) + d_ref[...] * u
        z_ref[t] = jax.nn.gelu(y)
    xr_ref[...] = xr
    xi_ref[...] = xi


def _s5_sample(u_tm, b_blk, c_blk, lbr, lbi, d_skip, x0r, x0i):
    n_steps, n_seq, d_ssm = u_tm.shape
    n_blk = b_blk.shape[0]
    nst = S5_BLOCK_STATE
    bc = S5_BLOCK_CH
    return pl.pallas_call(
        _s5_sample_kernel,
        grid=(n_blk,),
        in_specs=[
            pl.BlockSpec((n_steps, n_seq, bc), lambda g: (0, 0, g)),
            pl.BlockSpec((None, bc, 2 * nst), lambda g: (g, 0, 0)),
            pl.BlockSpec((None, 2 * nst, bc), lambda g: (g, 0, 0)),
            pl.BlockSpec((None, 1, nst), lambda g: (g, 0, 0)),
            pl.BlockSpec((None, 1, nst), lambda g: (g, 0, 0)),
            pl.BlockSpec((1, bc), lambda g: (0, g)),
            pl.BlockSpec((n_seq, nst), lambda g: (0, g)),
            pl.BlockSpec((n_seq, nst), lambda g: (0, g)),
        ],
        out_specs=[
            pl.BlockSpec((n_steps, n_seq, bc), lambda g: (0, 0, g)),
            pl.BlockSpec((n_seq, nst), lambda g: (0, g)),
            pl.BlockSpec((n_seq, nst), lambda g: (0, g)),
        ],
        out_shape=[
            jax.ShapeDtypeStruct((n_steps, n_seq, d_ssm), F32),
            jax.ShapeDtypeStruct(x0r.shape, F32),
            jax.ShapeDtypeStruct(x0i.shape, F32),
        ],
        compiler_params=_params("parallel"),
        name="s5_sample",
    )(u_tm, b_blk, c_blk, lbr.reshape(n_blk, 1, nst), lbi.reshape(n_blk, 1, nst), d_skip, x0r, x0i)


def _glu_kernel(z_ref, w_ref, b_ref, o_ref):
    z = z_ref[...]
    gate = jnp.dot(z.astype(BF16), w_ref[...], preferred_element_type=F32) + b_ref[...]
    o_ref[...] = (z * jax.nn.sigmoid(gate)).astype(o_ref.dtype)


def _glu(z, w_b, b, *, tm):
    T, N = z.shape
    return pl.pallas_call(
        _glu_kernel,
        grid=(T // tm,),
        in_specs=[
            pl.BlockSpec((tm, N), lambda i: (i, 0)),
            pl.BlockSpec((N, N), lambda i: (0, 0), pipeline_mode=pl.Buffered(1)),
            pl.BlockSpec((1, N), lambda i: (0, 0)),
        ],
        out_specs=pl.BlockSpec((tm, N), lambda i: (i, 0)),
        out_shape=jax.ShapeDtypeStruct((T, N), BF16),
        compiler_params=_params("parallel"),
        name="glu",
    )(z, w_b, b)


def _out_proj_kernel(a1_ref, a2_ref, w1_ref, w2_ref, x_ref, o_ref):
    acc = jnp.dot(a1_ref[...], w1_ref[...], preferred_element_type=F32)
    acc = acc + jnp.dot(a2_ref[...], w2_ref[...], preferred_element_type=F32)
    o_ref[...] = x_ref[...] + acc


def _out_proj(o_att, o_ssm, w_b, x2d, *, tm, tn):
    T, D = x2d.shape
    k1, k2 = o_att.shape[1], o_ssm.shape[1]
    assert k1 == k2
    return pl.pallas_call(
        _out_proj_kernel,
        grid=(T // tm, D // tn),
        in_specs=[
            pl.BlockSpec((tm, k1), lambda i, j: (i, 0)),
            pl.BlockSpec((tm, k2), lambda i, j: (i, 0)),
            pl.BlockSpec((k1, tn), lambda i, j: (0, j)),
            pl.BlockSpec((k2, tn), lambda i, j: (1, j)),
            pl.BlockSpec((tm, tn), lambda i, j: (i, j)),
        ],
        out_specs=pl.BlockSpec((tm, tn), lambda i, j: (i, j)),
        out_shape=jax.ShapeDtypeStruct((T, D), F32),
        compiler_params=_params("parallel", "arbitrary"),
        name="out_proj",
    )(o_att, o_ssm, w_b, w_b, x2d)


def _ffn_kernel(*refs, seq_len, has_state, has_tail, n_main, n_chunk):
    n_grp = 7 if has_state else 5
    x_ref, halo_ref, g_ref = refs[:3]
    groups = [refs[3:3 + n_grp]]
    n_out = 2
    if has_tail:
        groups.append(refs[3 + n_grp:3 + 2 * n_grp])
        n_out = 3
    outs = refs[3 + len(groups) * n_grp:]
    y_ref, a_refs, h_scr = outs[0], outs[1:n_out], outs[n_out]
    i = pl.program_id(0)
    f = pl.program_id(1)
    tm = x_ref.shape[0]

    @pl.when(f == 0)
    def _():
        _rms_rows_to(h_scr, 0, halo_ref, g_ref)
        _rms_rows_to(h_scr, FFN_HALO, x_ref, g_ref)
        y_ref[...] = x_ref[...]

    def columns(group, a_ref):
        wg_ref, wu_ref, wd_ref, cw_ref, cb_ref = group[:5]
        tf = wg_ref.shape[1]
        a_ext = jnp.dot(h_scr[...], wg_ref[...], preferred_element_type=F32)
        a = a_ext[FFN_HALO:]
        pos = (i * tm + lax.broadcasted_iota(jnp.int32, (tm, tf), 0)) % seq_len
        prev1 = a_ext[FFN_HALO - 1:FFN_HALO - 1 + tm]
        prev2 = a_ext[FFN_HALO - 2:FFN_HALO - 2 + tm]
        if has_state:
            prev1 = jnp.where(pos >= 1, prev1, group[5][...])
            prev2 = jnp.where(pos >= 2, prev2, group[6][...])
        else:
            prev1 = jnp.where(pos >= 1, prev1, 0.0)
            prev2 = jnp.where(pos >= 2, prev2, 0.0)
        conv = cb_ref[...] + cw_ref[0:1, :] * prev2 + cw_ref[1:2, :] * prev1 + cw_ref[2:3, :] * a
        up = jnp.dot(h_scr[FFN_HALO:, :], wu_ref[...], preferred_element_type=F32)
        gated = (jax.nn.silu(conv) * up).astype(BF16)
        d_out = y_ref.shape[1]
        for n in range(d_out // n_chunk):
            sl = slice(n * n_chunk, (n + 1) * n_chunk)
            y_ref[:, sl] += jnp.dot(gated, wd_ref[:, sl], preferred_element_type=F32)
        a_ref[...] = a if has_state else a[tm - SUBLANES:]

    if has_tail:
        pl.when(f < n_main)(lambda: columns(groups[0], a_refs[0]))
        pl.when(f == n_main)(lambda: columns(groups[1], a_refs[1]))
    else:
        columns(groups[0], a_refs[0])


def _ffn(x1, g2, wg_b, wu_b, wd_b, conv_w, conv_b, seq_len, e1=None, e2=None, *, tm, tf):
    T, D = x1.shape
    F = wg_b.shape[1]
    has_state = e1 is not None
    assert seq_len % tm == 0 or tm % seq_len == 0
    n_main, tw = divmod(F, tf)
    has_tail = tw > 0
    assert n_main > 0 and (not has_tail or (n_main * tf) % tw == 0)
    blocks_per_halo = tm // FFN_HALO
    a_rows, a_blk = (T, tm) if has_state else (T // tm * SUBLANES, SUBLANES)

    def group(width, col, **mode):
        specs = [
            pl.BlockSpec((D, width), lambda i, f: (0, col(f)), **mode),
            pl.BlockSpec((D, width), lambda i, f: (0, col(f)), **mode),
            pl.BlockSpec((width, D), lambda i, f: (col(f), 0), **mode),
            pl.BlockSpec((CONV_W, width), lambda i, f: (0, col(f))),
            pl.BlockSpec((1, width), lambda i, f: (0, col(f))),
        ]
        ops = [wg_b, wu_b, wd_b, conv_w, conv_b]
        if has_state:
            specs += [pl.BlockSpec((tm, width), lambda i, f: (i, col(f)))] * 2
            ops += [e1, e2]
        return specs, ops

    in_specs = [
        pl.BlockSpec((tm, D), lambda i, f: (i, 0), pipeline_mode=pl.Buffered(1)),
        pl.BlockSpec((FFN_HALO, D), lambda i, f: (jnp.maximum(i * blocks_per_halo - 1, 0), 0)),
        pl.BlockSpec((1, D), lambda i, f: (0, 0)),
    ]
    args = [x1, x1, g2]
    main_col = lambda f: jnp.minimum(f, n_main - 1)
    specs, ops = group(tf, main_col)
    in_specs += specs
    args += ops
    out_specs = [pl.BlockSpec((tm, D), lambda i, f: (i, 0), pipeline_mode=pl.Buffered(1)),
                 pl.BlockSpec((a_blk, tf), lambda i, f: (i, main_col(f)))]
    out_shape = [jax.ShapeDtypeStruct((T, D), F32), jax.ShapeDtypeStruct((a_rows, n_main * tf), F32)]
    if has_tail:
        tail_col = n_main * tf // tw
        specs, ops = group(tw, lambda f: tail_col, pipeline_mode=pl.Buffered(1))
        in_specs += specs
        args += ops
        out_specs.append(pl.BlockSpec((a_blk, tw), lambda i, f: (i, 0)))
        out_shape.append(jax.ShapeDtypeStruct((a_rows, tw), F32))
    kern = functools.partial(_ffn_kernel, seq_len=seq_len, has_state=has_state, has_tail=has_tail,
                             n_main=n_main, n_chunk=min(D, 512))
    outs = pl.pallas_call(
        kern,
        grid=(T // tm, n_main + int(has_tail)),
        in_specs=in_specs,
        out_specs=out_specs,
        out_shape=out_shape,
        scratch_shapes=[pltpu.VMEM((FFN_HALO + tm, D), BF16)],
        compiler_params=_params("parallel", "arbitrary"),
        name="ffn",
    )(*args)
    a = jnp.concatenate(outs[1:], axis=1)
    return outs[0], a


def _layer(x3d, lw, *, layer=0, cache=None, ssm0=None, conv0=None):
    (norm1_g, w_in, q_norm_g, k_norm_g, a_re, a_im, log_dt, b_re, b_im, c_re, c_im, ssm_d,
     w_glu, b_glu, w_out, norm2_g, w_gate, w_up, conv_w, conv_b, w_down) = lw
    nb, L, D = x3d.shape
    T = nb * L
    x2d = x3d.reshape(T, D)
    d_ssm = ssm_d.shape[0]
    d_att = (w_in.shape[1] - d_ssm) // 3
    n_heads = d_att // HEAD_DIM
    n_groups = a_re.shape[0]
    tm = min(PROJ_TM, T)
    tm_ffn = min(FFN_TM, T)

    head_gain = jnp.concatenate([jnp.tile(q_norm_g.astype(F32), n_heads),
                                 jnp.tile(k_norm_g.astype(F32), n_heads)])[None]
    proj, k, v = _in_proj(x2d, norm1_g[None], w_in.astype(BF16), head_gain, n_heads,
                          tm=min(IN_PROJ_TM, T), tn=SUBLANES * HEAD_DIM)
    k = k.reshape(nb, L, n_heads, HEAD_DIM)
    v = v.reshape(nb, L, n_heads, HEAD_DIM)

    b_blk, c_blk, lbr, lbi = _s5_pack(a_re, a_im, log_dt, b_re, b_im, c_re, c_im)
    d_skip = ssm_d.astype(F32)[None]
    if cache is None:
        o_att = _attn_prompt(proj, nb, L, n_heads)
        z, sre, sim = _s5_prompt(proj, nb, L, 3 * d_att // d_ssm, b_blk, c_blk, lbr, lbi, d_skip)
    else:
        o_att = _attn_sample(proj, cache[0], cache[1], layer, nb, n_heads).astype(BF16)
        u_tm = proj[:, 3 * d_att:].reshape(nb, L, d_ssm).transpose(1, 0, 2)
        z_tm, sre, sim = _s5_sample(u_tm, b_blk, c_blk, lbr, lbi, d_skip,
                                    ssm0[0].astype(F32).reshape(nb, n_groups * SSM_STATE),
                                    ssm0[1].astype(F32).reshape(nb, n_groups * SSM_STATE))
        z = z_tm.transpose(1, 0, 2).reshape(T, d_ssm)
    ssm_re = sre.reshape(nb, n_groups, SSM_STATE)
    ssm_im = sim.reshape(nb, n_groups, SSM_STATE)

    o_ssm = _glu(z, w_glu.astype(BF16), b_glu.astype(F32)[None], tm=tm)
    x1 = _out_proj(o_att, o_ssm, w_out.astype(BF16), x2d, tm=tm, tn=1024)

    ffn_w = (norm2_g[None], w_gate.astype(BF16), w_up.astype(BF16), w_down.astype(BF16),
             conv_w.astype(F32), conv_b.astype(F32)[None])
    F = w_gate.shape[1]
    if conv0 is None:
        y, a_tail = _ffn(x1, *ffn_w, L, tm=tm_ffn, tf=FFN_TF)
        tiles_per_seq = L // tm_ffn
        a_tail = a_tail.reshape(nb, tiles_per_seq, SUBLANES, F)
        conv_state = a_tail[:, -1, SUBLANES - (CONV_W - 1):]
    else:
        c0 = conv0.astype(F32)
        zeros = jnp.zeros((nb, L - 1, F), F32)
        e1 = jnp.concatenate([c0[:, 1:2], zeros], axis=1).reshape(T, F)
        e2 = jnp.concatenate([c0[:, 0:1], c0[:, 1:2], zeros[:, 1:]], axis=1).reshape(T, F)
        y, a_full = _ffn(x1, *ffn_w, L, e1, e2, tm=min(FFN_STATE_TM, T), tf=FFN_TF)
        conv_state = a_full.reshape(nb, L, F)[:, L - (CONV_W - 1):]
    return y.reshape(nb, L, D), k, v, ssm_re, ssm_im, conv_state


def kernel(x_prompt, x_sample, cache_k, cache_v, state_ssm_re, state_ssm_im, state_ffn_conv,
           norm1_g, w_in, q_norm_g, k_norm_g,
           ssm_a_re, ssm_a_im, ssm_log_dt, ssm_b_re, ssm_b_im, ssm_c_re, ssm_c_im,
           ssm_d, w_glu, b_glu, w_out, norm2_g,
           w_ffn_gate, w_ffn_up, ffn_conv_w, ffn_conv_b, w_ffn_down):
    depth = w_in.shape[0]
    seq = x_prompt.shape[1]
    wb_prompt = min(MAX_WINDOW, seq)
    outs_p = [[] for _ in range(5)]
    outs_s = [[] for _ in range(5)]
    yp, ys = x_prompt, x_sample
    for l in range(depth):
        lw = (norm1_g[l], w_in[l], q_norm_g[l], k_norm_g[l],
              ssm_a_re[l], ssm_a_im[l], ssm_log_dt[l], ssm_b_re[l], ssm_b_im[l], ssm_c_re[l], ssm_c_im[l],
              ssm_d[l], w_glu[l], b_glu[l], w_out[l], norm2_g[l],
              w_ffn_gate[l], w_ffn_up[l], ffn_conv_w[l], ffn_conv_b[l], w_ffn_down[l])
        yp, k_p, v_p, re_p, im_p, conv_p = _layer(yp, lw)
        ys, k_s, v_s, re_s, im_s, conv_s = _layer(
            ys, lw, layer=l, cache=(cache_k, cache_v),
            ssm0=(state_ssm_re[l], state_ssm_im[l]), conv0=state_ffn_conv[l])
        for lst, val in zip(outs_p, (k_p[:, seq - wb_prompt:], v_p[:, seq - wb_prompt:], re_p, im_p, conv_p)):
            lst.append(val)
        for lst, val in zip(outs_s, (k_s, v_s, re_s, im_s, conv_s)):
            lst.append(val)
    return (yp, ys, *(jnp.stack(o) for o in outs_p), *(jnp.stack(o) for o in outs_s))
```

```python
import functools
import math

import jax
import jax.numpy as jnp
import numpy as np
from jax import lax
from jax.experimental import pallas as pl
from jax.experimental.pallas import tpu as pltpu

F32 = jnp.float32
BF16 = jnp.bfloat16

HEAD_DIM = 128
DILATED_GROUPS = ((128, 1), (512, 4), (2048, 16))
MAX_WINDOW = max(w for w, _ in DILATED_GROUPS)
SSM_GROUP_CH = 16
SSM_STATE = 64
CONV_W = 3
NORM_EPS = 1e-6
MASKED = -1e30

LANES = 128
SUBLANES = 8
MXU_DIM = 256
VMEM_BYTES = 64 * 1024 * 1024
VMEM_LIMIT_BYTES = VMEM_BYTES - 4 * 1024 * 1024
FFN_VMEM_LIMIT_BYTES = VMEM_BYTES - 2 * 1024 * 1024

S5_GROUPS_PER_BLOCK = MXU_DIM // SSM_GROUP_CH
S5_BLOCK_CH = S5_GROUPS_PER_BLOCK * SSM_GROUP_CH
S5_BLOCK_STATE = S5_GROUPS_PER_BLOCK * SSM_STATE

ATTN_TQ = 256
ATTN_QBLOCKS = 2
S5_CHUNK = 256
PROJ_TM = 1024
IN_PROJ_TM = 512
FFN_TM = 512
FFN_STATE_TF = 256
FFN_TF = 512
FFN_HALO = 16


def _params(*sem, vmem_limit_bytes=VMEM_LIMIT_BYTES):
    return pltpu.CompilerParams(dimension_semantics=sem, vmem_limit_bytes=vmem_limit_bytes)


def _rms_rows(x, g):
    ms = jnp.mean(x * x, axis=-1, keepdims=True)
    return x * lax.rsqrt(ms + NORM_EPS) * g


def _rms_rows_to(dst_ref, dst_row0, src_ref, g_ref, chunk=128):
    rows = src_ref.shape[0]
    chunk = min(chunk, rows)

    def body(c, carry):
        r0 = pl.multiple_of(c * chunk, chunk)
        dst_ref[pl.ds(dst_row0 + r0, chunk), :] = _rms_rows(src_ref[pl.ds(r0, chunk), :], g_ref[...]).astype(BF16)
        return carry

    lax.fori_loop(0, rows // chunk, body, 0)


def _in_proj_kernel(x_ref, g_ref, w_ref, hg_ref, o_ref, ko_ref, vo_ref, h_scr, *, blocks_per_part):
    j = pl.program_id(1)
    nb = blocks_per_part
    heads = o_ref.shape[1] // HEAD_DIM
    tm = o_ref.shape[0]

    @pl.when(j == 0)
    def _():
        _rms_rows_to(h_scr, 0, x_ref, g_ref)

    acc = jnp.dot(h_scr[...], w_ref[...], preferred_element_type=F32)

    def head_normed(hh):
        sl = slice(hh * HEAD_DIM, (hh + 1) * HEAD_DIM)
        return sl, _rms_rows(acc[:, sl], hg_ref[:, sl])

    @pl.when(j < nb)
    def _():
        for hh in range(heads):
            sl, val = head_normed(hh)
            o_ref[:, sl] = val

    @pl.when(jnp.logical_and(j >= nb, j < 2 * nb))
    def _():
        for hh in range(heads):
            sl, val = head_normed(hh)
            o_ref[:, sl] = val
            ko_ref[pl.ds(hh, tm, stride=heads), :] = val

    @pl.when(jnp.logical_and(j >= 2 * nb, j < 3 * nb))
    def _():
        o_ref[...] = acc
        for hh in range(heads):
            vo_ref[pl.ds(hh, tm, stride=heads), :] = acc[:, hh * HEAD_DIM:(hh + 1) * HEAD_DIM]

    @pl.when(j >= 3 * nb)
    def _():
        o_ref[...] = acc


def _in_proj(x2d, g1, w_b, head_gain, n_heads, *, tm, tn):
    T, D = x2d.shape
    N = w_b.shape[1]
    assert tn == SUBLANES * HEAD_DIM and (n_heads * HEAD_DIM) % tn == 0
    nb = n_heads * HEAD_DIM // tn
    kern = functools.partial(_in_proj_kernel, blocks_per_part=nb)
    kv_shape = jax.ShapeDtypeStruct((nb, T * SUBLANES, HEAD_DIM), F32)
    return pl.pallas_call(
        kern,
        grid=(T // tm, N // tn),
        in_specs=[
            pl.BlockSpec((tm, D), lambda i, j: (i, 0), pipeline_mode=pl.Buffered(1)),
            pl.BlockSpec((1, D), lambda i, j: (0, 0)),
            pl.BlockSpec((D, tn), lambda i, j: (0, j)),
            pl.BlockSpec((1, tn), lambda i, j: (0, jnp.minimum(j, 2 * nb - 1))),
        ],
        out_specs=[
            pl.BlockSpec((tm, tn), lambda i, j: (i, j)),
            pl.BlockSpec((None, tm * SUBLANES, HEAD_DIM), lambda i, j: (jnp.clip(j - nb, 0, nb - 1), i, 0)),
            pl.BlockSpec((None, tm * SUBLANES, HEAD_DIM), lambda i, j: (jnp.clip(j - 2 * nb, 0, nb - 1), i, 0)),
        ],
        out_shape=[jax.ShapeDtypeStruct((T, N), F32), kv_shape, kv_shape],
        scratch_shapes=[pltpu.VMEM((tm, D), BF16)],
        compiler_params=_params("parallel", "arbitrary"),
        name="in_proj",
    )(x2d, g1, w_b, head_gain)


LOG2E = math.log2(math.e)


def _multiplicity(d):
    count = np.zeros(d.shape, np.int32)
    for window, dilation in DILATED_GROUPS:
        count += ((d >= 0) & (d <= window) & (d % dilation == 0)).astype(np.int32)
    return count


def _bias2(d, slope=0.0):
    count = _multiplicity(d)
    return np.where(count > 0, np.log2(np.maximum(count, 1)) - slope * LOG2E * d, MASKED).astype(np.float32)


def _alibi_slopes(n_heads):
    return (2.0 ** (-8.0 * np.arange(1, n_heads + 1, dtype=np.float64) / n_heads)).astype(np.float32)


def _attn_prompt_kernel(slopes_ref, q_ref, k_ref, v_ref, logc_ref, o_ref, kb_scr, vb_scr, bias_scr,
                        *, n_off):
    h = pl.program_id(1)
    qi = pl.program_id(2)
    tq = bias_scr.shape[0]
    pad = (n_off - 1) * tq

    @pl.when(qi == 0)
    def _():
        kb_scr[0:pad, :] = jnp.zeros((pad, HEAD_DIM), BF16)
        vb_scr[0:pad, :] = jnp.zeros((pad, HEAD_DIM), BF16)
        kb_scr[pad:, :] = k_ref[...].astype(BF16)
        vb_scr[pad:, :] = v_ref[...].astype(BF16)
        slope = slopes_ref[h]
        row = lax.broadcasted_iota(jnp.int32, (tq, tq), 0)
        col = lax.broadcasted_iota(jnp.int32, (tq, tq), 1)
        base = (row - col).astype(F32)
        for j in range(n_off):
            m = n_off - 1 - j
            bias_scr[:, j * tq:(j + 1) * tq] = logc_ref[m] - (slope * LOG2E) * (base + float(m * tq))

    n_sub = q_ref.shape[0] // tq

    def blocks(masked):
        for sub in range(n_sub):
            _attn_prompt_block(qi * n_sub + sub, q_ref.at[sub * tq:(sub + 1) * tq],
                               o_ref.at[sub * tq:(sub + 1) * tq], kb_scr, vb_scr, bias_scr,
                               n_off=n_off, masked=masked)

    pl.when(qi * n_sub >= n_off - 1)(lambda: blocks(False))
    pl.when(qi * n_sub < n_off - 1)(lambda: blocks(True))


def _attn_prompt_block(qb, q_ref, o_ref, kb_scr, vb_scr, bias_scr, *, n_off, masked):
    tq = q_ref.shape[0]
    wlen = n_off * tq
    q = (q_ref[...] * (HEAD_DIM ** -0.5 * LOG2E)).astype(BF16)
    start = pl.multiple_of(qb * tq, tq)
    kw = kb_scr[pl.ds(start, wlen), :]
    vw = vb_scr[pl.ds(start, wlen), :]
    s = lax.dot_general(q, kw, (((1,), (1,)), ((), ())), preferred_element_type=F32) + bias_scr[...]
    if masked:
        lane = lax.broadcasted_iota(jnp.int32, (1, wlen), 1)
        s = s + jnp.where(lane < (n_off - 1 - qb) * tq, MASKED, 0.0)
    m_i = jnp.max(s, axis=-1, keepdims=True)
    p = jnp.exp2(s - m_i)
    l_i = jnp.sum(p, axis=-1, keepdims=True)
    acc = jnp.dot(p.astype(BF16), vw, preferred_element_type=F32)
    o_ref[...] = (acc / l_i).astype(o_ref.dtype)


def _attn_prompt(proj, n_batch, seq, n_heads):
    tq = ATTN_TQ
    rows = ATTN_QBLOCKS * tq
    nq = seq // rows
    n_off = MAX_WINDOW // tq + 1
    m = np.arange(n_off)[:, None, None]
    r = np.arange(tq)[None, :, None]
    c = np.arange(tq)[None, None, :]
    logc = _bias2(m * tq + r - c)
    kern = functools.partial(_attn_prompt_kernel, n_off=n_off)
    pad = (n_off - 1) * tq
    return pl.pallas_call(
        kern,
        grid=(n_batch, n_heads, nq),
        in_specs=[
            pl.BlockSpec(memory_space=pltpu.SMEM),
            pl.BlockSpec((rows, HEAD_DIM), lambda b, h, i: (b * nq + i, h)),
            pl.BlockSpec((seq, HEAD_DIM), lambda b, h, i: (b, n_heads + h)),
            pl.BlockSpec((seq, HEAD_DIM), lambda b, h, i: (b, 2 * n_heads + h)),
            pl.BlockSpec((n_off, tq, tq), lambda b, h, i: (0, 0, 0)),
        ],
        out_specs=pl.BlockSpec((rows, HEAD_DIM), lambda b, h, i: (b * nq + i, h)),
        out_shape=jax.ShapeDtypeStruct((n_batch * seq, n_heads * HEAD_DIM), BF16),
        scratch_shapes=[pltpu.VMEM((pad + seq, HEAD_DIM), BF16), pltpu.VMEM((pad + seq, HEAD_DIM), BF16),
                        pltpu.VMEM((tq, n_off * tq), F32)],
        compiler_params=_params("parallel", "parallel", "arbitrary"),
        name="attn_prompt",
    )(_alibi_slopes(n_heads), proj, proj, proj, logc)


def _attn_sample_kernel(q_ref, kn_ref, vn_ref, kf_ref, kr_ref, vf_ref, vr_ref,
                        bf_ref, br_ref, bn_ref, o_ref, *, heads):
    n_new = q_ref.shape[0]
    nt = (((1,), (1,)), ((), ()))

    def by_head(ref):
        return jnp.concatenate([ref[:, h * HEAD_DIM:(h + 1) * HEAD_DIM] for h in range(heads)], axis=0)

    q = (by_head(q_ref) * (HEAD_DIM ** -0.5 * LOG2E)).astype(BF16)
    pad = jnp.zeros((LANES - heads * n_new, HEAD_DIM), F32)
    kn = jnp.concatenate([by_head(kn_ref), pad], axis=0).astype(BF16)
    vn = jnp.concatenate([by_head(vn_ref), pad], axis=0).astype(BF16)
    kf = kf_ref[...].reshape(-1, HEAD_DIM).astype(BF16)
    kr = kr_ref[...].reshape(-1, HEAD_DIM).astype(BF16)
    s_f = lax.dot_general(q, kf, nt, preferred_element_type=F32) + bf_ref[...]
    s_r = lax.dot_general(q, kr, nt, preferred_element_type=F32) + br_ref[...]
    s_n = lax.dot_general(q, kn, nt, preferred_element_type=F32) + bn_ref[...]
    m = jnp.maximum(jnp.maximum(jnp.max(s_f, axis=-1, keepdims=True), jnp.max(s_r, axis=-1, keepdims=True)),
                    jnp.max(s_n, axis=-1, keepdims=True))
    p_f = jnp.exp2(s_f - m)
    p_r = jnp.exp2(s_r - m)
    p_n = jnp.exp2(s_n - m)
    l = (jnp.sum(p_f, axis=-1, keepdims=True) + jnp.sum(p_r, axis=-1, keepdims=True)
         + jnp.sum(p_n, axis=-1, keepdims=True))
    vf = vf_ref[...].reshape(-1, HEAD_DIM).astype(BF16)
    vr = vr_ref[...].reshape(-1, HEAD_DIM).astype(BF16)
    o = (jnp.dot(p_f.astype(BF16), vf, preferred_element_type=F32)
         + jnp.dot(p_r.astype(BF16), vr, preferred_element_type=F32)
         + jnp.dot(p_n.astype(BF16), vn, preferred_element_type=F32)) / l
    for h in range(heads):
        o_ref[:, h * HEAD_DIM:(h + 1) * HEAD_DIM] = o[h * n_new:(h + 1) * n_new]


def _attn_sample(proj, cache_k, cache_v, layer, n_seq, n_heads):
    depth, _, win, _, _ = cache_k.shape
    n_new = proj.shape[0] // n_seq
    hs = SUBLANES
    nhg = n_heads // hs
    (w_far, dil), (w_mid, _) = sorted(DILATED_GROUPS, reverse=True)[:2]
    assert n_new == SUBLANES and win % dil == 0 and (win - w_mid) % dil == 0 and win >= w_far >= w_mid
    n_chunk = win // dil
    far_chunks = (win - w_mid) // dil
    rec_chunks = n_chunk - far_chunks
    assert far_chunks % rec_chunks == 0
    kc = cache_k.reshape(depth * n_seq, n_chunk, dil, n_heads, HEAD_DIM)
    vc = cache_v.reshape(depth * n_seq, n_chunk, dil, n_heads, HEAD_DIM)

    slopes = _alibi_slopes(n_heads).astype(np.float64)
    same_head = np.arange(hs)[:, None, None, None, None] == np.arange(hs)[None, None, None, None, :]
    i = np.arange(n_new)[None, :, None, None, None]

    def table(chunks, rows):
        r = (chunks[:, None] * dil + rows[None, :])[None, None, :, :, None]
        d = np.broadcast_to(win + i - r, (hs, n_new, len(chunks), len(rows), hs))
        out = [np.where(same_head, _bias2(d, slopes[g * hs:(g + 1) * hs, None, None, None, None]), MASKED)
               for g in range(nhg)]
        return np.stack(out).reshape(nhg, hs * n_new, -1).astype(np.float32)

    chunks = np.arange(n_chunk)
    rows = np.arange(dil)
    bias_f = table(chunks[:far_chunks], rows[:n_new])
    bias_r = table(chunks[far_chunks:], rows)
    skipped_rows = (chunks[:far_chunks, None] * dil + rows[None, n_new:]).reshape(-1)
    assert not _multiplicity(win + np.arange(n_new)[:, None] - skipped_rows[None, :]).any()
    j = np.arange(LANES)
    hj, tj = j // n_new, j % n_new
    d_n = np.broadcast_to(np.arange(n_new)[None, :, None] - tj[None, None, :], (hs, n_new, LANES))
    own = hj[None, None, :] == np.arange(hs)[:, None, None]
    bias_n = np.stack([np.where(own, _bias2(d_n, slopes[g * hs:(g + 1) * hs, None, None]), MASKED)
                       for g in range(nhg)]).reshape(nhg, hs * n_new, LANES).astype(np.float32)

    wb = hs * HEAD_DIM
    q_blocks = n_heads * HEAD_DIM // wb
    base = layer * n_seq
    kern = functools.partial(_attn_sample_kernel, heads=hs)
    far_spec = pl.BlockSpec((None, far_chunks, n_new, hs, HEAD_DIM), lambda b, g: (base + b, 0, 0, g, 0))
    rec_spec = pl.BlockSpec((None, rec_chunks, dil, hs, HEAD_DIM),
                            lambda b, g: (base + b, far_chunks // rec_chunks, 0, g, 0))
    return pl.pallas_call(
        kern,
        grid=(n_seq, nhg),
        in_specs=[
            pl.BlockSpec((n_new, wb), lambda b, g: (b, g)),
            pl.BlockSpec((n_new, wb), lambda b, g: (b, q_blocks + g)),
            pl.BlockSpec((n_new, wb), lambda b, g: (b, 2 * q_blocks + g)),
            far_spec, rec_spec, far_spec, rec_spec,
            pl.BlockSpec((None,) + bias_f.shape[1:], lambda b, g: (g, 0, 0)),
            pl.BlockSpec((None,) + bias_r.shape[1:], lambda b, g: (g, 0, 0)),
            pl.BlockSpec((None,) + bias_n.shape[1:], lambda b, g: (g, 0, 0)),
        ],
        out_specs=pl.BlockSpec((n_new, wb), lambda b, g: (b, g)),
        out_shape=jax.ShapeDtypeStruct((n_seq * n_new, n_heads * HEAD_DIM), F32),
        compiler_params=_params("parallel", "arbitrary"),
        name="attn_sample",
    )(proj, proj, proj, kc, kc, vc, vc, bias_f, bias_r, bias_n)


def _s5_pack(a_re, a_im, log_dt, b_re, b_im, c_re, c_im):
    dt = jnp.exp(log_dt.astype(F32))[:, None]
    ar, ai = a_re.astype(F32), a_im.astype(F32)
    mag = jnp.exp(ar * dt)
    lb_re, lb_im = mag * jnp.cos(ai * dt), mag * jnp.sin(ai * dt)
    den = ar * ar + ai * ai
    ir, ii = ar / den, -ai / den
    cr = (lb_re - 1.0) * ir - lb_im * ii
    ci = (lb_re - 1.0) * ii + lb_im * ir
    bb_re = cr[..., None] * b_re - ci[..., None] * b_im
    bb_im = cr[..., None] * b_im + ci[..., None] * b_re
    n_blk = a_re.shape[0] // S5_GROUPS_PER_BLOCK
    gb, p, ch = S5_GROUPS_PER_BLOCK, SSM_STATE, SSM_GROUP_CH
    eye = jnp.eye(gb, dtype=F32)

    def pack_in(bb):
        return jnp.einsum('aGpc,GH->aGcHp', bb.reshape(n_blk, gb, p, ch), eye).reshape(n_blk, gb * ch, gb * p)

    def pack_out(cc):
        return jnp.einsum('aGcp,GH->aHpGc', cc.reshape(n_blk, gb, ch, p), eye).reshape(n_blk, gb * p, gb * ch)

    b_blk = jnp.concatenate([pack_in(bb_re), pack_in(bb_im)], axis=2).astype(BF16)
    c_blk = jnp.concatenate([pack_out(c_re.astype(F32)), -pack_out(c_im.astype(F32))], axis=1).astype(BF16)
    lbr = lb_re.reshape(n_blk, gb * p)
    lbi = lb_im.reshape(n_blk, gb * p)
    return b_blk, c_blk, lbr, lbi


def _s5_prompt_kernel(u_ref, b_ref, c_ref, lbr_ref, lbi_ref, d_ref, z_ref, sre_ref, sim_ref,
                      s_scr, xr_scr, xi_scr):
    i = pl.program_id(1)
    lc = u_ref.shape[0]
    n_blk = b_ref.shape[0]
    n_slab = s_scr.shape[0]
    half = n_slab // 2

    @pl.when(i == 0)
    def _():
        xr_scr[...] = jnp.zeros_like(xr_scr)
        xi_scr[...] = jnp.zeros_like(xi_scr)

    u = u_ref[...]
    ub = u.astype(BF16)
    for g in range(n_blk):
        bu = jnp.dot(ub[:, g * S5_BLOCK_CH:(g + 1) * S5_BLOCK_CH], b_ref[g], preferred_element_type=F32)
        for s in range(n_slab):
            s_scr[s, pl.ds(g, lc, stride=n_blk), :] = bu[:, s * LANES:(s + 1) * LANES]

    def step(t, carry):
        row = pl.multiple_of(t * n_blk, n_blk)
        new = []
        for s in range(half):
            xr, xi = carry[s], carry[half + s]
            lr = lbr_ref[:, s * LANES:(s + 1) * LANES]
            li = lbi_ref[:, s * LANES:(s + 1) * LANES]
            nr = lr * xr - li * xi + s_scr[s, pl.ds(row, n_blk), :]
            ni = lr * xi + li * xr + s_scr[half + s, pl.ds(row, n_blk), :]
            s_scr[s, pl.ds(row, n_blk), :] = nr
            s_scr[half + s, pl.ds(row, n_blk), :] = ni
            new.append((nr, ni))
        return tuple(n[0] for n in new) + tuple(n[1] for n in new)

    init = tuple(xr_scr[:, s * LANES:(s + 1) * LANES] for s in range(half)) + \
        tuple(xi_scr[:, s * LANES:(s + 1) * LANES] for s in range(half))
    fin = lax.fori_loop(0, lc, step, init, unroll=4)
    for s in range(half):
        xr_scr[:, s * LANES:(s + 1) * LANES] = fin[s]
        xi_scr[:, s * LANES:(s + 1) * LANES] = fin[half + s]
    sre_ref[...] = xr_scr[...]
    sim_ref[...] = xi_scr[...]

    for g in range(n_blk):
        xs = jnp.concatenate([s_scr[s, pl.ds(g, lc, stride=n_blk), :] for s in range(n_slab)], axis=1)
        y = jnp.dot(xs.astype(BF16), c_ref[g], preferred_element_type=F32)
        sl = slice(g * S5_BLOCK_CH, (g + 1) * S5_BLOCK_CH)
        y = y + d_ref[:, sl] * u[:, sl]
        z_ref[:, sl] = jax.nn.gelu(y)


def _s5_prompt(proj, n_batch, seq, u_col_block, b_blk, c_blk, lbr, lbi, d_skip, *, lc=S5_CHUNK):
    n_blk = b_blk.shape[0]
    assert n_blk == SUBLANES
    d_ssm = n_blk * S5_BLOCK_CH
    nst = S5_BLOCK_STATE
    nc = seq // lc
    n_slab = 2 * nst // LANES
    z, sre, sim = pl.pallas_call(
        _s5_prompt_kernel,
        grid=(n_batch, nc),
        in_specs=[
            pl.BlockSpec((lc, d_ssm), lambda b, i: (b * nc + i, u_col_block)),
            pl.BlockSpec(b_blk.shape, lambda b, i: (0, 0, 0), pipeline_mode=pl.Buffered(1)),
            pl.BlockSpec(c_blk.shape, lambda b, i: (0, 0, 0), pipeline_mode=pl.Buffered(1)),
            pl.BlockSpec(lbr.shape, lambda b, i: (0, 0)),
            pl.BlockSpec(lbi.shape, lambda b, i: (0, 0)),
            pl.BlockSpec((1, d_ssm), lambda b, i: (0, 0)),
        ],
        out_specs=[
            pl.BlockSpec((lc, d_ssm), lambda b, i: (b * nc + i, 0)),
            pl.BlockSpec((None, n_blk, nst), lambda b, i: (b, 0, 0)),
            pl.BlockSpec((None, n_blk, nst), lambda b, i: (b, 0, 0)),
        ],
        out_shape=[
            jax.ShapeDtypeStruct((n_batch * seq, d_ssm), F32),
            jax.ShapeDtypeStruct((n_batch, n_blk, nst), F32),
            jax.ShapeDtypeStruct((n_batch, n_blk, nst), F32),
        ],
        scratch_shapes=[pltpu.VMEM((n_slab, lc * n_blk, LANES), F32),
                        pltpu.VMEM((n_blk, nst), F32), pltpu.VMEM((n_blk, nst), F32)],
        compiler_params=_params("parallel", "arbitrary"),
        name="s5_prompt",
    )(proj, b_blk, c_blk, lbr, lbi, d_skip)
    return z, sre, sim


def _s5_sample_kernel(u_ref, b_ref, c_ref, lbr_ref, lbi_ref, d_ref, x0r_ref, x0i_ref,
                      z_ref, xr_ref, xi_ref):
    n_steps = u_ref.shape[0]
    lr = lbr_ref[...]
    li = lbi_ref[...]
    xr = x0r_ref[...]
    xi = x0i_ref[...]
    nst = xr.shape[1]
    for t in range(n_steps):
        u = u_ref[t]
        bu = jnp.dot(u.astype(BF16), b_ref[...], preferred_element_type=F32)
        xr, xi = lr * xr - li * xi + bu[:, :nst], lr * xi + li * xr + bu[:, nst:]
        xs = jnp.concatenate([xr, xi], axis=1).astype(BF16)
        y = jnp.dot(xs, c_ref[...], preferred_element_type=F32) + d_ref[...] * u
        z_ref[t] = jax.nn.gelu(y)
    xr_ref[...] = xr
    xi_ref[...] = xi


def _s5_sample(u_tm, b_blk, c_blk, lbr, lbi, d_skip, x0r, x0i):
    n_steps, n_seq, d_ssm = u_tm.shape
    n_blk = b_blk.shape[0]
    nst = S5_BLOCK_STATE
    bc = S5_BLOCK_CH
    return pl.pallas_call(
        _s5_sample_kernel,
        grid=(n_blk,),
        in_specs=[
            pl.BlockSpec((n_steps, n_seq, bc), lambda g: (0, 0, g)),
            pl.BlockSpec((None, bc, 2 * nst), lambda g: (g, 0, 0)),
            pl.BlockSpec((None, 2 * nst, bc), lambda g: (g, 0, 0)),
            pl.BlockSpec((None, 1, nst), lambda g: (g, 0, 0)),
            pl.BlockSpec((None, 1, nst), lambda g: (g, 0, 0)),
            pl.BlockSpec((1, bc), lambda g: (0, g)),
            pl.BlockSpec((n_seq, nst), lambda g: (0, g)),
            pl.BlockSpec((n_seq, nst), lambda g: (0, g)),
        ],
        out_specs=[
            pl.BlockSpec((n_steps, n_seq, bc), lambda g: (0, 0, g)),
            pl.BlockSpec((n_seq, nst), lambda g: (0, g)),
            pl.BlockSpec((n_seq, nst), lambda g: (0, g)),
        ],
        out_shape=[
            jax.ShapeDtypeStruct((n_steps, n_seq, d_ssm), F32),
            jax.ShapeDtypeStruct(x0r.shape, F32),
            jax.ShapeDtypeStruct(x0i.shape, F32),
        ],
        compiler_params=_params("parallel"),
        name="s5_sample",
    )(u_tm, b_blk, c_blk, lbr.reshape(n_blk, 1, nst), lbi.reshape(n_blk, 1, nst), d_skip, x0r, x0i)


def _glu_kernel(z_ref, w_ref, b_ref, o_ref):
    z = z_ref[...]
    gate = jnp.dot(z.astype(BF16), w_ref[...], preferred_element_type=F32) + b_ref[...]
    o_ref[...] = (z * jax.nn.sigmoid(gate)).astype(o_ref.dtype)


def _glu(z, w_b, b, *, tm):
    T, N = z.shape
    return pl.pallas_call(
        _glu_kernel,
        grid=(T // tm,),
        in_specs=[
            pl.BlockSpec((tm, N), lambda i: (i, 0)),
            pl.BlockSpec((N, N), lambda i: (0, 0), pipeline_mode=pl.Buffered(1)),
            pl.BlockSpec((1, N), lambda i: (0, 0)),
        ],
        out_specs=pl.BlockSpec((tm, N), lambda i: (i, 0)),
        out_shape=jax.ShapeDtypeStruct((T, N), BF16),
        compiler_params=_params("parallel"),
        name="glu",
    )(z, w_b, b)


def _out_proj_kernel(a1_ref, a2_ref, w1_ref, w2_ref, x_ref, o_ref):
    acc = jnp.dot(a1_ref[...], w1_ref[...], preferred_element_type=F32)
    acc = acc + jnp.dot(a2_ref[...], w2_ref[...], preferred_element_type=F32)
    o_ref[...] = x_ref[...] + acc


def _out_proj(o_att, o_ssm, w_b, x2d, *, tm, tn):
    T, D = x2d.shape
    k1, k2 = o_att.shape[1], o_ssm.shape[1]
    assert k1 == k2
    return pl.pallas_call(
        _out_proj_kernel,
        grid=(T // tm, D // tn),
        in_specs=[
            pl.BlockSpec((tm, k1), lambda i, j: (i, 0)),
            pl.BlockSpec((tm, k2), lambda i, j: (i, 0)),
            pl.BlockSpec((k1, tn), lambda i, j: (0, j)),
            pl.BlockSpec((k2, tn), lambda i, j: (1, j)),
            pl.BlockSpec((tm, tn), lambda i, j: (i, j)),
        ],
        out_specs=pl.BlockSpec((tm, tn), lambda i, j: (i, j)),
        out_shape=jax.ShapeDtypeStruct((T, D), F32),
        compiler_params=_params("parallel", "arbitrary"),
        name="out_proj",
    )(o_att, o_ssm, w_b, w_b, x2d)


def _ffn_kernel(*refs, seq_len, has_state, has_tail, n_main, n_chunk):
    n_grp = 7 if has_state else 5
    x_ref, halo_ref, g_ref = refs[:3]
    groups = [refs[3:3 + n_grp]]
    n_out = 2
    if has_tail:
        groups.append(refs[3 + n_grp:3 + 2 * n_grp])
        n_out = 3
    outs = refs[3 + len(groups) * n_grp:]
    y_ref, a_refs, h_scr, gated_scr = outs[0], outs[1:n_out], outs[n_out], outs[n_out + 1]
    i = pl.program_id(0)
    s = pl.program_id(1)
    tm = x_ref.shape[0]
    n_tiles = n_main + int(has_tail)
    cur = s % 2

    @pl.when(s == 0)
    def _():
        _rms_rows_to(h_scr, 0, halo_ref, g_ref)
        _rms_rows_to(h_scr, FFN_HALO, x_ref, g_ref)
        y_ref[...] = x_ref[...]

    def gate_up(group, a_ref):
        wg_ref, wu_ref, _, cw_ref, cb_ref = group[:5]
        tf = wg_ref.shape[1]
        a_ext = jnp.dot(h_scr[...], wg_ref[...], preferred_element_type=F32)
        a = a_ext[FFN_HALO:]
        pos = (i * tm + lax.broadcasted_iota(jnp.int32, (tm, tf), 0)) % seq_len
        prev1 = a_ext[FFN_HALO - 1:FFN_HALO - 1 + tm]
        prev2 = a_ext[FFN_HALO - 2:FFN_HALO - 2 + tm]
        if has_state:
            prev1 = jnp.where(pos >= 1, prev1, group[5][...])
            prev2 = jnp.where(pos >= 2, prev2, group[6][...])
        else:
            prev1 = jnp.where(pos >= 1, prev1, 0.0)
            prev2 = jnp.where(pos >= 2, prev2, 0.0)
        conv = cb_ref[...] + cw_ref[0:1, :] * prev2 + cw_ref[1:2, :] * prev1 + cw_ref[2:3, :] * a
        up = jnp.dot(h_scr[FFN_HALO:, :], wu_ref[...], preferred_element_type=F32)
        gated_scr[cur, :, 0:tf] = (jax.nn.silu(conv) * up).astype(BF16)
        a_ref[...] = a if has_state else a[tm - SUBLANES:]

    def down(group):
        wd_ref = group[2]
        gated = gated_scr[1 - cur, :, 0:wd_ref.shape[0]]
        d_out = y_ref.shape[1]
        for n in range(d_out // n_chunk):
            sl = slice(n * n_chunk, (n + 1) * n_chunk)
            y_ref[:, sl] += jnp.dot(gated, wd_ref[:, sl], preferred_element_type=F32)

    main, last = groups[0], groups[-1]

    @pl.when(s == 0)
    def _():
        gate_up(main, a_refs[0])

    @pl.when(jnp.logical_and(s >= 1, s < n_main))
    def _():
        gate_up(main, a_refs[0])
        down(main)

    if has_tail:
        @pl.when(s == n_main)
        def _():
            gate_up(last, a_refs[1])
            down(main)

    @pl.when(s == n_tiles)
    def _():
        down(last)


def _ffn(x1, g2, wg_b, wu_b, wd_b, conv_w, conv_b, seq_len, e1=None, e2=None, *, tm, tf):
    T, D = x1.shape
    F = wg_b.shape[1]
    has_state = e1 is not None
    assert seq_len % tm == 0 or tm % seq_len == 0
    n_main, tw = divmod(F, tf)
    has_tail = tw > 0
    assert n_main > 1 and (not has_tail or (n_main * tf) % tw == 0)
    blocks_per_halo = tm // FFN_HALO
    a_rows, a_blk = (T, tm) if has_state else (T // tm * SUBLANES, SUBLANES)

    def group(width, col, col_down, **mode):
        specs = [
            pl.BlockSpec((D, width), lambda i, s: (0, col(s)), **mode),
            pl.BlockSpec((D, width), lambda i, s: (0, col(s)), **mode),
            pl.BlockSpec((width, D), lambda i, s: (col_down(s), 0), **mode),
            pl.BlockSpec((CONV_W, width), lambda i, s: (0, col(s))),
            pl.BlockSpec((1, width), lambda i, s: (0, col(s))),
        ]
        ops = [wg_b, wu_b, wd_b, conv_w, conv_b]
        if has_state:
            specs += [pl.BlockSpec((tm, width), lambda i, s: (i, col(s)))] * 2
            ops += [e1, e2]
        return specs, ops

    in_specs = [
        pl.BlockSpec((tm, D), lambda i, s: (i, 0), pipeline_mode=pl.Buffered(1)),
        pl.BlockSpec((FFN_HALO, D), lambda i, s: (jnp.maximum(i * blocks_per_halo - 1, 0), 0),
                     pipeline_mode=pl.Buffered(1)),
        pl.BlockSpec((1, D), lambda i, s: (0, 0)),
    ]
    args = [x1, x1, g2]
    main_col = lambda s: jnp.minimum(s, n_main - 1)
    specs, ops = group(tf, main_col, lambda s: jnp.clip(s - 1, 0, n_main - 1))
    in_specs += specs
    args += ops
    out_specs = [pl.BlockSpec((tm, D), lambda i, s: (i, 0), pipeline_mode=pl.Buffered(1)),
                 pl.BlockSpec((a_blk, tf), lambda i, s: (i, main_col(s)))]
    out_shape = [jax.ShapeDtypeStruct((T, D), F32), jax.ShapeDtypeStruct((a_rows, n_main * tf), F32)]
    if has_tail:
        tail_col = n_main * tf // tw
        specs, ops = group(tw, lambda s: tail_col, lambda s: tail_col, pipeline_mode=pl.Buffered(1))
        in_specs += specs
        args += ops
        out_specs.append(pl.BlockSpec((a_blk, tw), lambda i, s: (i, 0)))
        out_shape.append(jax.ShapeDtypeStruct((a_rows, tw), F32))
    kern = functools.partial(_ffn_kernel, seq_len=seq_len, has_state=has_state, has_tail=has_tail,
                             n_main=n_main, n_chunk=min(D, 512))
    outs = pl.pallas_call(
        kern,
        grid=(T // tm, n_main + int(has_tail) + 1),
        in_specs=in_specs,
        out_specs=out_specs,
        out_shape=out_shape,
        scratch_shapes=[pltpu.VMEM((FFN_HALO + tm, D), BF16), pltpu.VMEM((2, tm, tf), BF16)],
        compiler_params=_params("parallel", "arbitrary", vmem_limit_bytes=FFN_VMEM_LIMIT_BYTES),
        name="ffn",
    )(*args)
    a = jnp.concatenate(outs[1:], axis=1)
    return outs[0], a


def _layer(x3d, lw, *, layer=0, cache=None, ssm0=None, conv0=None):
    (norm1_g, w_in, q_norm_g, k_norm_g, a_re, a_im, log_dt, b_re, b_im, c_re, c_im, ssm_d,
     w_glu, b_glu, w_out, norm2_g, w_gate, w_up, conv_w, conv_b, w_down) = lw
    nb, L, D = x3d.shape
    T = nb * L
    x2d = x3d.reshape(T, D)
    d_ssm = ssm_d.shape[0]
    d_att = (w_in.shape[1] - d_ssm) // 3
    n_heads = d_att // HEAD_DIM
    n_groups = a_re.shape[0]
    tm = min(PROJ_TM, T)
    tm_ffn = min(FFN_TM, T)

    head_gain = jnp.concatenate([jnp.tile(q_norm_g.astype(F32), n_heads),
                                 jnp.tile(k_norm_g.astype(F32), n_heads)])[None]
    proj, k, v = _in_proj(x2d, norm1_g[None], w_in.astype(BF16), head_gain, n_heads,
                          tm=min(IN_PROJ_TM, T), tn=SUBLANES * HEAD_DIM)
    keep = min(MAX_WINDOW, L) if cache is None else L

    def head_major(kv):
        kv = kv.reshape(-1, nb, L, SUBLANES, HEAD_DIM)[:, :, L - keep:]
        return kv.transpose(1, 2, 0, 3, 4).reshape(nb, keep, n_heads, HEAD_DIM)

    k, v = head_major(k), head_major(v)

    b_blk, c_blk, lbr, lbi = _s5_pack(a_re, a_im, log_dt, b_re, b_im, c_re, c_im)
    d_skip = ssm_d.astype(F32)[None]
    if cache is None:
        o_att = _attn_prompt(proj, nb, L, n_heads)
        z, sre, sim = _s5_prompt(proj, nb, L, 3 * d_att // d_ssm, b_blk, c_blk, lbr, lbi, d_skip)
    else:
        o_att = _attn_sample(proj, cache[0], cache[1], layer, nb, n_heads).astype(BF16)
        u_tm = proj[:, 3 * d_att:].reshape(nb, L, d_ssm).transpose(1, 0, 2)
        z_tm, sre, sim = _s5_sample(u_tm, b_blk, c_blk, lbr, lbi, d_skip,
                                    ssm0[0].astype(F32).reshape(nb, n_groups * SSM_STATE),
                                    ssm0[1].astype(F32).reshape(nb, n_groups * SSM_STATE))
        z = z_tm.transpose(1, 0, 2).reshape(T, d_ssm)
    ssm_re = sre.reshape(nb, n_groups, SSM_STATE)
    ssm_im = sim.reshape(nb, n_groups, SSM_STATE)

    o_ssm = _glu(z, w_glu.astype(BF16), b_glu.astype(F32)[None], tm=tm)
    x1 = _out_proj(o_att, o_ssm, w_out.astype(BF16), x2d, tm=tm, tn=1024)

    ffn_w = (norm2_g[None], w_gate.astype(BF16), w_up.astype(BF16), w_down.astype(BF16),
             conv_w.astype(F32), conv_b.astype(F32)[None])
    F = w_gate.shape[1]
    if conv0 is None:
        y, a_tail = _ffn(x1, *ffn_w, L, tm=tm_ffn, tf=FFN_TF)
        tiles_per_seq = L // tm_ffn
        a_tail = a_tail.reshape(nb, tiles_per_seq, SUBLANES, F)
        conv_state = a_tail[:, -1, SUBLANES - (CONV_W - 1):]
    else:
        c0 = conv0.astype(F32)
        zeros = jnp.zeros((nb, L - 1, F), F32)
        e1 = jnp.concatenate([c0[:, 1:2], zeros], axis=1).reshape(T, F)
        e2 = jnp.concatenate([c0[:, 0:1], c0[:, 1:2], zeros[:, 1:]], axis=1).reshape(T, F)
        y, a_full = _ffn(x1, *ffn_w, L, e1, e2, tm=tm_ffn, tf=FFN_STATE_TF)
        conv_state = a_full.reshape(nb, L, F)[:, L - (CONV_W - 1):]
    return y.reshape(nb, L, D), k, v, ssm_re, ssm_im, conv_state


def kernel(x_prompt, x_sample, cache_k, cache_v, state_ssm_re, state_ssm_im, state_ffn_conv,
           norm1_g, w_in, q_norm_g, k_norm_g,
           ssm_a_re, ssm_a_im, ssm_log_dt, ssm_b_re, ssm_b_im, ssm_c_re, ssm_c_im,
           ssm_d, w_glu, b_glu, w_out, norm2_g,
           w_ffn_gate, w_ffn_up, ffn_conv_w, ffn_conv_b, w_ffn_down):
    depth = w_in.shape[0]
    outs_p = [[] for _ in range(5)]
    outs_s = [[] for _ in range(5)]
    yp, ys = x_prompt, x_sample
    for l in range(depth):
        lw = (norm1_g[l], w_in[l], q_norm_g[l], k_norm_g[l],
              ssm_a_re[l], ssm_a_im[l], ssm_log_dt[l], ssm_b_re[l], ssm_b_im[l], ssm_c_re[l], ssm_c_im[l],
              ssm_d[l], w_glu[l], b_glu[l], w_out[l], norm2_g[l],
              w_ffn_gate[l], w_ffn_up[l], ffn_conv_w[l], ffn_conv_b[l], w_ffn_down[l])
        yp, k_p, v_p, re_p, im_p, conv_p = _layer(yp, lw)
        ys, k_s, v_s, re_s, im_s, conv_s = _layer(
            ys, lw, layer=l, cache=(cache_k, cache_v),
            ssm0=(state_ssm_re[l], state_ssm_im[l]), conv0=state_ffn_conv[l])
        for lst, val in zip(outs_p, (k_p, v_p, re_p, im_p, conv_p)):
            lst.append(val)
        for lst, val in zip(outs_s, (k_s, v_s, re_s, im_s, conv_s)):
            lst.append(val)
    return (yp, ys, *(jnp.stack(o) for o in outs_p), *(jnp.stack(o) for o in outs_s))
```

```python
import functools
import math

import jax
import jax.numpy as jnp
import numpy as np
from jax import lax
from jax.experimental import pallas as pl
from jax.experimental.pallas import tpu as pltpu

F32 = jnp.float32
BF16 = jnp.bfloat16

HEAD_DIM = 128
DILATED_GROUPS = ((128, 1), (512, 4), (2048, 16))
MAX_WINDOW = max(w for w, _ in DILATED_GROUPS)
SSM_GROUP_CH = 16
SSM_STATE = 64
CONV_W = 3
NORM_EPS = 1e-6
MASKED = -1e30

LANES = 128
SUBLANES = 8
MXU_DIM = 256
VMEM_BYTES = 64 * 1024 * 1024
VMEM_LIMIT_BYTES = VMEM_BYTES - 4 * 1024 * 1024
FFN_VMEM_LIMIT_BYTES = VMEM_BYTES - 2 * 1024 * 1024

S5_GROUPS_PER_BLOCK = MXU_DIM // SSM_GROUP_CH
S5_BLOCK_CH = S5_GROUPS_PER_BLOCK * SSM_GROUP_CH
S5_BLOCK_STATE = S5_GROUPS_PER_BLOCK * SSM_STATE

ATTN_TQ = 256
ATTN_QBLOCKS = 2
S5_CHUNK = 256
PROJ_TM = 1024
IN_PROJ_TM = 512
FFN_TM = 512
FFN_STATE_TF = 256
FFN_TF = 512
FFN_HALO = 16


def _params(*sem, vmem_limit_bytes=VMEM_LIMIT_BYTES):
    return pltpu.CompilerParams(dimension_semantics=sem, vmem_limit_bytes=vmem_limit_bytes)


def _rms_rows(x, g):
    ms = jnp.mean(x * x, axis=-1, keepdims=True)
    return x * lax.rsqrt(ms + NORM_EPS) * g


def _rms_rows_to(dst_ref, dst_row0, src_ref, g_ref, chunk=128):
    rows = src_ref.shape[0]
    chunk = min(chunk, rows)

    def body(c, carry):
        r0 = pl.multiple_of(c * chunk, chunk)
        dst_ref[pl.ds(dst_row0 + r0, chunk), :] = _rms_rows(src_ref[pl.ds(r0, chunk), :], g_ref[...]).astype(BF16)
        return carry

    lax.fori_loop(0, rows // chunk, body, 0)


def _in_proj_kernel(x_ref, g_ref, w_ref, hg_ref, o_ref, ko_ref, vo_ref, h_scr, *, blocks_per_part):
    j = pl.program_id(1)
    nb = blocks_per_part
    heads = o_ref.shape[1] // HEAD_DIM
    tm = o_ref.shape[0]

    @pl.when(j == 0)
    def _():
        _rms_rows_to(h_scr, 0, x_ref, g_ref)

    acc = jnp.dot(h_scr[...], w_ref[...], preferred_element_type=F32)

    def head_normed(hh):
        sl = slice(hh * HEAD_DIM, (hh + 1) * HEAD_DIM)
        return sl, _rms_rows(acc[:, sl], hg_ref[:, sl])

    @pl.when(j < nb)
    def _():
        for hh in range(heads):
            sl, val = head_normed(hh)
            o_ref[:, sl] = val

    @pl.when(jnp.logical_and(j >= nb, j < 2 * nb))
    def _():
        for hh in range(heads):
            sl, val = head_normed(hh)
            o_ref[:, sl] = val
            ko_ref[pl.ds(hh, tm, stride=heads), :] = val

    @pl.when(jnp.logical_and(j >= 2 * nb, j < 3 * nb))
    def _():
        o_ref[...] = acc
        for hh in range(heads):
            vo_ref[pl.ds(hh, tm, stride=heads), :] = acc[:, hh * HEAD_DIM:(hh + 1) * HEAD_DIM]

    @pl.when(j >= 3 * nb)
    def _():
        o_ref[...] = acc


def _in_proj(x2d, g1, w_b, head_gain, n_heads, *, tm, tn):
    T, D = x2d.shape
    N = w_b.shape[1]
    assert tn == SUBLANES * HEAD_DIM and (n_heads * HEAD_DIM) % tn == 0
    nb = n_heads * HEAD_DIM // tn
    kern = functools.partial(_in_proj_kernel, blocks_per_part=nb)
    kv_shape = jax.ShapeDtypeStruct((nb, T * SUBLANES, HEAD_DIM), F32)
    return pl.pallas_call(
        kern,
        grid=(T // tm, N // tn),
        in_specs=[
            pl.BlockSpec((tm, D), lambda i, j: (i, 0), pipeline_mode=pl.Buffered(1)),
            pl.BlockSpec((1, D), lambda i, j: (0, 0)),
            pl.BlockSpec((D, tn), lambda i, j: (0, j)),
            pl.BlockSpec((1, tn), lambda i, j: (0, jnp.minimum(j, 2 * nb - 1))),
        ],
        out_specs=[
            pl.BlockSpec((tm, tn), lambda i, j: (i, j)),
            pl.BlockSpec((None, tm * SUBLANES, HEAD_DIM), lambda i, j: (jnp.clip(j - nb, 0, nb - 1), i, 0)),
            pl.BlockSpec((None, tm * SUBLANES, HEAD_DIM), lambda i, j: (jnp.clip(j - 2 * nb, 0, nb - 1), i, 0)),
        ],
        out_shape=[jax.ShapeDtypeStruct((T, N), F32), kv_shape, kv_shape],
        scratch_shapes=[pltpu.VMEM((tm, D), BF16)],
        compiler_params=_params("parallel", "arbitrary"),
        name="in_proj",
    )(x2d, g1, w_b, head_gain)


LOG2E = math.log2(math.e)


def _multiplicity(d):
    count = np.zeros(d.shape, np.int32)
    for window, dilation in DILATED_GROUPS:
        count += ((d >= 0) & (d <= window) & (d % dilation == 0)).astype(np.int32)
    return count


def _bias2(d, slope=0.0):
    count = _multiplicity(d)
    return np.where(count > 0, np.log2(np.maximum(count, 1)) - slope * LOG2E * d, MASKED).astype(np.float32)


def _alibi_slopes(n_heads):
    return (2.0 ** (-8.0 * np.arange(1, n_heads + 1, dtype=np.float64) / n_heads)).astype(np.float32)


def _attn_prompt_kernel(slopes_ref, q_ref, k_ref, v_ref, logc_ref, o_ref, kb_scr, vb_scr, bias_scr,
                        *, n_off):
    h = pl.program_id(1)
    qi = pl.program_id(2)
    tq = bias_scr.shape[0]
    pad = (n_off - 1) * tq

    @pl.when(qi == 0)
    def _():
        kb_scr[0:pad, :] = jnp.zeros((pad, HEAD_DIM), BF16)
        vb_scr[0:pad, :] = jnp.zeros((pad, HEAD_DIM), BF16)
        kb_scr[pad:, :] = k_ref[...].astype(BF16)
        vb_scr[pad:, :] = v_ref[...].astype(BF16)
        slope = slopes_ref[h]
        row = lax.broadcasted_iota(jnp.int32, (tq, tq), 0)
        col = lax.broadcasted_iota(jnp.int32, (tq, tq), 1)
        base = (row - col).astype(F32)
        for j in range(n_off):
            m = n_off - 1 - j
            bias_scr[:, j * tq:(j + 1) * tq] = logc_ref[m] - (slope * LOG2E) * (base + float(m * tq))

    n_sub = q_ref.shape[0] // tq

    def blocks(masked):
        for sub in range(n_sub):
            _attn_prompt_block(qi * n_sub + sub, q_ref.at[sub * tq:(sub + 1) * tq],
                               o_ref.at[sub * tq:(sub + 1) * tq], kb_scr, vb_scr, bias_scr,
                               n_off=n_off, masked=masked)

    pl.when(qi * n_sub >= n_off - 1)(lambda: blocks(False))
    pl.when(qi * n_sub < n_off - 1)(lambda: blocks(True))


def _attn_prompt_block(qb, q_ref, o_ref, kb_scr, vb_scr, bias_scr, *, n_off, masked):
    tq = q_ref.shape[0]
    wlen = n_off * tq
    q = (q_ref[...] * (HEAD_DIM ** -0.5 * LOG2E)).astype(BF16)
    start = pl.multiple_of(qb * tq, tq)
    kw = kb_scr[pl.ds(start, wlen), :]
    vw = vb_scr[pl.ds(start, wlen), :]
    s = lax.dot_general(q, kw, (((1,), (1,)), ((), ())), preferred_element_type=F32) + bias_scr[...]
    if masked:
        lane = lax.broadcasted_iota(jnp.int32, (1, wlen), 1)
        s = s + jnp.where(lane < (n_off - 1 - qb) * tq, MASKED, 0.0)
    m_i = jnp.max(s, axis=-1, keepdims=True)
    p = jnp.exp2(s - m_i)
    l_i = jnp.sum(p, axis=-1, keepdims=True)
    acc = jnp.dot(p.astype(BF16), vw, preferred_element_type=F32)
    o_ref[...] = (acc / l_i).astype(o_ref.dtype)


def _attn_prompt(proj, n_batch, seq, n_heads):
    tq = ATTN_TQ
    rows = ATTN_QBLOCKS * tq
    nq = seq // rows
    n_off = MAX_WINDOW // tq + 1
    m = np.arange(n_off)[:, None, None]
    r = np.arange(tq)[None, :, None]
    c = np.arange(tq)[None, None, :]
    logc = _bias2(m * tq + r - c)
    kern = functools.partial(_attn_prompt_kernel, n_off=n_off)
    pad = (n_off - 1) * tq
    return pl.pallas_call(
        kern,
        grid=(n_batch, n_heads, nq),
        in_specs=[
            pl.BlockSpec(memory_space=pltpu.SMEM),
            pl.BlockSpec((rows, HEAD_DIM), lambda b, h, i: (b * nq + i, h)),
            pl.BlockSpec((seq, HEAD_DIM), lambda b, h, i: (b, n_heads + h)),
            pl.BlockSpec((seq, HEAD_DIM), lambda b, h, i: (b, 2 * n_heads + h)),
            pl.BlockSpec((n_off, tq, tq), lambda b, h, i: (0, 0, 0)),
        ],
        out_specs=pl.BlockSpec((rows, HEAD_DIM), lambda b, h, i: (b * nq + i, h)),
        out_shape=jax.ShapeDtypeStruct((n_batch * seq, n_heads * HEAD_DIM), BF16),
        scratch_shapes=[pltpu.VMEM((pad + seq, HEAD_DIM), BF16), pltpu.VMEM((pad + seq, HEAD_DIM), BF16),
                        pltpu.VMEM((tq, n_off * tq), F32)],
        compiler_params=_params("parallel", "parallel", "arbitrary"),
        name="attn_prompt",
    )(_alibi_slopes(n_heads), proj, proj, proj, logc)


def _attn_sample_kernel(q_ref, kn_ref, vn_ref, kf_ref, kr_ref, vf_ref, vr_ref,
                        bf_ref, br_ref, bn_ref, o_ref, *, heads):
    n_new = q_ref.shape[0]
    nt = (((1,), (1,)), ((), ()))

    def by_head(ref):
        return jnp.concatenate([ref[:, h * HEAD_DIM:(h + 1) * HEAD_DIM] for h in range(heads)], axis=0)

    q = (by_head(q_ref) * (HEAD_DIM ** -0.5 * LOG2E)).astype(BF16)
    pad = jnp.zeros((LANES - heads * n_new, HEAD_DIM), F32)
    kn = jnp.concatenate([by_head(kn_ref), pad], axis=0).astype(BF16)
    vn = jnp.concatenate([by_head(vn_ref), pad], axis=0).astype(BF16)
    kf = kf_ref[...].reshape(-1, HEAD_DIM).astype(BF16)
    kr = kr_ref[...].reshape(-1, HEAD_DIM).astype(BF16)
    s_f = lax.dot_general(q, kf, nt, preferred_element_type=F32) + bf_ref[...]
    s_r = lax.dot_general(q, kr, nt, preferred_element_type=F32) + br_ref[...]
    s_n = lax.dot_general(q, kn, nt, preferred_element_type=F32) + bn_ref[...]
    m = jnp.maximum(jnp.maximum(jnp.max(s_f, axis=-1, keepdims=True), jnp.max(s_r, axis=-1, keepdims=True)),
                    jnp.max(s_n, axis=-1, keepdims=True))
    p_f = jnp.exp2(s_f - m)
    p_r = jnp.exp2(s_r - m)
    p_n = jnp.exp2(s_n - m)
    l = (jnp.sum(p_f, axis=-1, keepdims=True) + jnp.sum(p_r, axis=-1, keepdims=True)
         + jnp.sum(p_n, axis=-1, keepdims=True))
    vf = vf_ref[...].reshape(-1, HEAD_DIM).astype(BF16)
    vr = vr_ref[...].reshape(-1, HEAD_DIM).astype(BF16)
    o = (jnp.dot(p_f.astype(BF16), vf, preferred_element_type=F32)
         + jnp.dot(p_r.astype(BF16), vr, preferred_element_type=F32)
         + jnp.dot(p_n.astype(BF16), vn, preferred_element_type=F32)) / l
    for h in range(heads):
        o_ref[:, h * HEAD_DIM:(h + 1) * HEAD_DIM] = o[h * n_new:(h + 1) * n_new]


def _attn_sample(proj, cache_k, cache_v, layer, n_seq, n_heads):
    depth, _, win, _, _ = cache_k.shape
    n_new = proj.shape[0] // n_seq
    hs = SUBLANES
    nhg = n_heads // hs
    (w_far, dil), (w_mid, _) = sorted(DILATED_GROUPS, reverse=True)[:2]
    assert n_new == SUBLANES and win % dil == 0 and (win - w_mid) % dil == 0 and win >= w_far >= w_mid
    n_chunk = win // dil
    far_chunks = (win - w_mid) // dil
    rec_chunks = n_chunk - far_chunks
    assert far_chunks % rec_chunks == 0
    kc = cache_k.reshape(depth * n_seq, n_chunk, dil, n_heads, HEAD_DIM)
    vc = cache_v.reshape(depth * n_seq, n_chunk, dil, n_heads, HEAD_DIM)

    slopes = _alibi_slopes(n_heads).astype(np.float64)
    same_head = np.arange(hs)[:, None, None, None, None] == np.arange(hs)[None, None, None, None, :]
    i = np.arange(n_new)[None, :, None, None, None]

    def table(chunks, rows):
        r = (chunks[:, None] * dil + rows[None, :])[None, None, :, :, None]
        d = np.broadcast_to(win + i - r, (hs, n_new, len(chunks), len(rows), hs))
        out = [np.where(same_head, _bias2(d, slopes[g * hs:(g + 1) * hs, None, None, None, None]), MASKED)
               for g in range(nhg)]
        return np.stack(out).reshape(nhg, hs * n_new, -1).astype(np.float32)

    chunks = np.arange(n_chunk)
    rows = np.arange(dil)
    bias_f = table(chunks[:far_chunks], rows[:n_new])
    bias_r = table(chunks[far_chunks:], rows)
    skipped_rows = (chunks[:far_chunks, None] * dil + rows[None, n_new:]).reshape(-1)
    assert not _multiplicity(win + np.arange(n_new)[:, None] - skipped_rows[None, :]).any()
    j = np.arange(LANES)
    hj, tj = j // n_new, j % n_new
    d_n = np.broadcast_to(np.arange(n_new)[None, :, None] - tj[None, None, :], (hs, n_new, LANES))
    own = hj[None, None, :] == np.arange(hs)[:, None, None]
    bias_n = np.stack([np.where(own, _bias2(d_n, slopes[g * hs:(g + 1) * hs, None, None]), MASKED)
                       for g in range(nhg)]).reshape(nhg, hs * n_new, LANES).astype(np.float32)

    wb = hs * HEAD_DIM
    q_blocks = n_heads * HEAD_DIM // wb
    base = layer * n_seq
    kern = functools.partial(_attn_sample_kernel, heads=hs)
    far_spec = pl.BlockSpec((None, far_chunks, n_new, hs, HEAD_DIM), lambda g, b: (base + b, 0, 0, g, 0))
    rec_spec = pl.BlockSpec((None, rec_chunks, dil, hs, HEAD_DIM),
                            lambda g, b: (base + b, far_chunks // rec_chunks, 0, g, 0))
    return pl.pallas_call(
        kern,
        grid=(nhg, n_seq),
        in_specs=[
            pl.BlockSpec((n_new, wb), lambda g, b: (b, g)),
            pl.BlockSpec((n_new, wb), lambda g, b: (b, q_blocks + g)),
            pl.BlockSpec((n_new, wb), lambda g, b: (b, 2 * q_blocks + g)),
            far_spec, rec_spec, far_spec, rec_spec,
            pl.BlockSpec((None,) + bias_f.shape[1:], lambda g, b: (g, 0, 0)),
            pl.BlockSpec((None,) + bias_r.shape[1:], lambda g, b: (g, 0, 0)),
            pl.BlockSpec((None,) + bias_n.shape[1:], lambda g, b: (g, 0, 0)),
        ],
        out_specs=pl.BlockSpec((n_new, wb), lambda g, b: (b, g)),
        out_shape=jax.ShapeDtypeStruct((n_seq * n_new, n_heads * HEAD_DIM), F32),
        compiler_params=_params("parallel", "parallel"),
        name="attn_sample",
    )(proj, proj, proj, kc, kc, vc, vc, bias_f, bias_r, bias_n)


def _s5_pack(a_re, a_im, log_dt, b_re, b_im, c_re, c_im):
    dt = jnp.exp(log_dt.astype(F32))[:, None]
    ar, ai = a_re.astype(F32), a_im.astype(F32)
    mag = jnp.exp(ar * dt)
    lb_re, lb_im = mag * jnp.cos(ai * dt), mag * jnp.sin(ai * dt)
    den = ar * ar + ai * ai
    ir, ii = ar / den, -ai / den
    cr = (lb_re - 1.0) * ir - lb_im * ii
    ci = (lb_re - 1.0) * ii + lb_im * ir
    bb_re = cr[..., None] * b_re - ci[..., None] * b_im
    bb_im = cr[..., None] * b_im + ci[..., None] * b_re
    n_blk = a_re.shape[0] // S5_GROUPS_PER_BLOCK
    gb, p, ch = S5_GROUPS_PER_BLOCK, SSM_STATE, SSM_GROUP_CH
    eye = jnp.eye(gb, dtype=F32)

    def pack_in(bb):
        return jnp.einsum('aGpc,GH->aGcHp', bb.reshape(n_blk, gb, p, ch), eye).reshape(n_blk, gb * ch, gb * p)

    def pack_out(cc):
        return jnp.einsum('aGcp,GH->aHpGc', cc.reshape(n_blk, gb, ch, p), eye).reshape(n_blk, gb * p, gb * ch)

    b_blk = jnp.concatenate([pack_in(bb_re), pack_in(bb_im)], axis=2).astype(BF16)
    c_blk = jnp.concatenate([pack_out(c_re.astype(F32)), -pack_out(c_im.astype(F32))], axis=1).astype(BF16)
    lbr = lb_re.reshape(n_blk, gb * p)
    lbi = lb_im.reshape(n_blk, gb * p)
    return b_blk, c_blk, lbr, lbi


def _s5_prompt_kernel(u_ref, b_ref, c_ref, lbr_ref, lbi_ref, d_ref, z_ref, sre_ref, sim_ref,
                      s_scr, xr_scr, xi_scr):
    i = pl.program_id(1)
    lc = u_ref.shape[0]
    n_blk = b_ref.shape[0]
    n_slab = s_scr.shape[0]
    half = n_slab // 2

    @pl.when(i == 0)
    def _():
        xr_scr[...] = jnp.zeros_like(xr_scr)
        xi_scr[...] = jnp.zeros_like(xi_scr)

    u = u_ref[...]
    ub = u.astype(BF16)
    for g in range(n_blk):
        bu = jnp.dot(ub[:, g * S5_BLOCK_CH:(g + 1) * S5_BLOCK_CH], b_ref[g], preferred_element_type=F32)
        for s in range(n_slab):
            s_scr[s, pl.ds(g, lc, stride=n_blk), :] = bu[:, s * LANES:(s + 1) * LANES]

    def step(t, carry):
        row = pl.multiple_of(t * n_blk, n_blk)
        new = []
        for s in range(half):
            xr, xi = carry[s], carry[half + s]
            lr = lbr_ref[:, s * LANES:(s + 1) * LANES]
            li = lbi_ref[:, s * LANES:(s + 1) * LANES]
            nr = lr * xr - li * xi + s_scr[s, pl.ds(row, n_blk), :]
            ni = lr * xi + li * xr + s_scr[half + s, pl.ds(row, n_blk), :]
            s_scr[s, pl.ds(row, n_blk), :] = nr
            s_scr[half + s, pl.ds(row, n_blk), :] = ni
            new.append((nr, ni))
        return tuple(n[0] for n in new) + tuple(n[1] for n in new)

    init = tuple(xr_scr[:, s * LANES:(s + 1) * LANES] for s in range(half)) + \
        tuple(xi_scr[:, s * LANES:(s + 1) * LANES] for s in range(half))
    fin = lax.fori_loop(0, lc, step, init, unroll=4)
    for s in range(half):
        xr_scr[:, s * LANES:(s + 1) * LANES] = fin[s]
        xi_scr[:, s * LANES:(s + 1) * LANES] = fin[half + s]
    sre_ref[...] = xr_scr[...]
    sim_ref[...] = xi_scr[...]

    for g in range(n_blk):
        xs = jnp.concatenate([s_scr[s, pl.ds(g, lc, stride=n_blk), :] for s in range(n_slab)], axis=1)
        y = jnp.dot(xs.astype(BF16), c_ref[g], preferred_element_type=F32)
        sl = slice(g * S5_BLOCK_CH, (g + 1) * S5_BLOCK_CH)
        y = y + d_ref[:, sl] * u[:, sl]
        z_ref[:, sl] = jax.nn.gelu(y)


def _s5_prompt(proj, n_batch, seq, u_col_block, b_blk, c_blk, lbr, lbi, d_skip, *, lc=S5_CHUNK):
    n_blk = b_blk.shape[0]
    assert n_blk == SUBLANES
    d_ssm = n_blk * S5_BLOCK_CH
    nst = S5_BLOCK_STATE
    nc = seq // lc
    n_slab = 2 * nst // LANES
    z, sre, sim = pl.pallas_call(
        _s5_prompt_kernel,
        grid=(n_batch, nc),
        in_specs=[
            pl.BlockSpec((lc, d_ssm), lambda b, i: (b * nc + i, u_col_block)),
            pl.BlockSpec(b_blk.shape, lambda b, i: (0, 0, 0), pipeline_mode=pl.Buffered(1)),
            pl.BlockSpec(c_blk.shape, lambda b, i: (0, 0, 0), pipeline_mode=pl.Buffered(1)),
            pl.BlockSpec(lbr.shape, lambda b, i: (0, 0)),
            pl.BlockSpec(lbi.shape, lambda b, i: (0, 0)),
            pl.BlockSpec((1, d_ssm), lambda b, i: (0, 0)),
        ],
        out_specs=[
            pl.BlockSpec((lc, d_ssm), lambda b, i: (b * nc + i, 0)),
            pl.BlockSpec((None, n_blk, nst), lambda b, i: (b, 0, 0)),
            pl.BlockSpec((None, n_blk, nst), lambda b, i: (b, 0, 0)),
        ],
        out_shape=[
            jax.ShapeDtypeStruct((n_batch * seq, d_ssm), F32),
            jax.ShapeDtypeStruct((n_batch, n_blk, nst), F32),
            jax.ShapeDtypeStruct((n_batch, n_blk, nst), F32),
        ],
        scratch_shapes=[pltpu.VMEM((n_slab, lc * n_blk, LANES), F32),
                        pltpu.VMEM((n_blk, nst), F32), pltpu.VMEM((n_blk, nst), F32)],
        compiler_params=_params("parallel", "arbitrary"),
        name="s5_prompt",
    )(proj, b_blk, c_blk, lbr, lbi, d_skip)
    return z, sre, sim


def _s5_sample_kernel(u_ref, b_ref, c_ref, lbr_ref, lbi_ref, d_ref, x0r_ref, x0i_ref,
                      z_ref, xr_ref, xi_ref):
    n_steps = u_ref.shape[0]
    lr = lbr_ref[...]
    li = lbi_ref[...]
    xr = x0r_ref[...]
    xi = x0i_ref[...]
    nst = xr.shape[1]
    for t in range(n_steps):
        u = u_ref[t]
        bu = jnp.dot(u.astype(BF16), b_ref[...], preferred_element_type=F32)
        xr, xi = lr * xr - li * xi + bu[:, :nst], lr * xi + li * xr + bu[:, nst:]
        xs = jnp.concatenate([xr, xi], axis=1).astype(BF16)
        y = jnp.dot(xs, c_ref[...], preferred_element_type=F32) + d_ref[...] * u
        z_ref[t] = jax.nn.gelu(y)
    xr_ref[...] = xr
    xi_ref[...] = xi


def _s5_sample(u_tm, b_blk, c_blk, lbr, lbi, d_skip, x0r, x0i):
    n_steps, n_seq, d_ssm = u_tm.shape
    n_blk = b_blk.shape[0]
    nst = S5_BLOCK_STATE
    bc = S5_BLOCK_CH
    return pl.pallas_call(
        _s5_sample_kernel,
        grid=(n_blk,),
        in_specs=[
            pl.BlockSpec((n_steps, n_seq, bc), lambda g: (0, 0, g)),
            pl.BlockSpec((None, bc, 2 * nst), lambda g: (g, 0, 0)),
            pl.BlockSpec((None, 2 * nst, bc), lambda g: (g, 0, 0)),
            pl.BlockSpec((None, 1, nst), lambda g: (g, 0, 0)),
            pl.BlockSpec((None, 1, nst), lambda g: (g, 0, 0)),
            pl.BlockSpec((1, bc), lambda g: (0, g)),
            pl.BlockSpec((n_seq, nst), lambda g: (0, g)),
            pl.BlockSpec((n_seq, nst), lambda g: (0, g)),
        ],
        out_specs=[
            pl.BlockSpec((n_steps, n_seq, bc), lambda g: (0, 0, g)),
            pl.BlockSpec((n_seq, nst), lambda g: (0, g)),
            pl.BlockSpec((n_seq, nst), lambda g: (0, g)),
        ],
        out_shape=[
            jax.ShapeDtypeStruct((n_steps, n_seq, d_ssm), F32),
            jax.ShapeDtypeStruct(x0r.shape, F32),
            jax.ShapeDtypeStruct(x0i.shape, F32),
        ],
        compiler_params=_params("parallel"),
        name="s5_sample",
    )(u_tm, b_blk, c_blk, lbr.reshape(n_blk, 1, nst), lbi.reshape(n_blk, 1, nst), d_skip, x0r, x0i)


def _glu_kernel(z_ref, w_ref, b_ref, o_ref):
    z = z_ref[...]
    gate = jnp.dot(z.astype(BF16), w_ref[...], preferred_element_type=F32) + b_ref[...]
    o_ref[...] = (z * jax.nn.sigmoid(gate)).astype(o_ref.dtype)


def _glu(z, w_b, b, *, tm):
    T, N = z.shape
    return pl.pallas_call(
        _glu_kernel,
        grid=(T // tm,),
        in_specs=[
            pl.BlockSpec((tm, N), lambda i: (i, 0)),
            pl.BlockSpec((N, N), lambda i: (0, 0), pipeline_mode=pl.Buffered(1)),
            pl.BlockSpec((1, N), lambda i: (0, 0)),
        ],
        out_specs=pl.BlockSpec((tm, N), lambda i: (i, 0)),
        out_shape=jax.ShapeDtypeStruct((T, N), BF16),
        compiler_params=_params("parallel"),
        name="glu",
    )(z, w_b, b)


def _out_proj_kernel(a1_ref, a2_ref, w1_ref, w2_ref, x_ref, o_ref):
    acc = jnp.dot(a1_ref[...], w1_ref[...], preferred_element_type=F32)
    acc = acc + jnp.dot(a2_ref[...], w2_ref[...], preferred_element_type=F32)
    o_ref[...] = x_ref[...] + acc


def _out_proj(o_att, o_ssm, w_b, x2d, *, tm, tn):
    T, D = x2d.shape
    k1, k2 = o_att.shape[1], o_ssm.shape[1]
    assert k1 == k2
    return pl.pallas_call(
        _out_proj_kernel,
        grid=(T // tm, D // tn),
        in_specs=[
            pl.BlockSpec((tm, k1), lambda i, j: (i, 0)),
            pl.BlockSpec((tm, k2), lambda i, j: (i, 0)),
            pl.BlockSpec((k1, tn), lambda i, j: (0, j)),
            pl.BlockSpec((k2, tn), lambda i, j: (1, j)),
            pl.BlockSpec((tm, tn), lambda i, j: (i, j)),
        ],
        out_specs=pl.BlockSpec((tm, tn), lambda i, j: (i, j)),
        out_shape=jax.ShapeDtypeStruct((T, D), F32),
        compiler_params=_params("parallel", "arbitrary"),
        name="out_proj",
    )(o_att, o_ssm, w_b, w_b, x2d)


def _ffn_kernel(*refs, seq_len, has_state, has_tail, n_main, n_chunk):
    n_grp = 7 if has_state else 5
    x_ref, halo_ref, g_ref = refs[:3]
    groups = [refs[3:3 + n_grp]]
    n_out = 2
    if has_tail:
        groups.append(refs[3 + n_grp:3 + 2 * n_grp])
        n_out = 3
    outs = refs[3 + len(groups) * n_grp:]
    y_ref, a_refs, h_scr = outs[0], outs[1:n_out], outs[n_out]
    i = pl.program_id(0)
    f = pl.program_id(1)
    tm = x_ref.shape[0]

    @pl.when(f == 0)
    def _():
        _rms_rows_to(h_scr, 0, halo_ref, g_ref)
        _rms_rows_to(h_scr, FFN_HALO, x_ref, g_ref)
        y_ref[...] = x_ref[...]

    def columns(group, a_ref):
        wg_ref, wu_ref, wd_ref, cw_ref, cb_ref = group[:5]
        tf = wg_ref.shape[1]
        a_ext = jnp.dot(h_scr[...], wg_ref[...], preferred_element_type=F32)
        a = a_ext[FFN_HALO:]
        pos = (i * tm + lax.broadcasted_iota(jnp.int32, (tm, tf), 0)) % seq_len
        prev1 = a_ext[FFN_HALO - 1:FFN_HALO - 1 + tm]
        prev2 = a_ext[FFN_HALO - 2:FFN_HALO - 2 + tm]
        if has_state:
            prev1 = jnp.where(pos >= 1, prev1, group[5][...])
            prev2 = jnp.where(pos >= 2, prev2, group[6][...])
        else:
            prev1 = jnp.where(pos >= 1, prev1, 0.0)
            prev2 = jnp.where(pos >= 2, prev2, 0.0)
        conv = cb_ref[...] + cw_ref[0:1, :] * prev2 + cw_ref[1:2, :] * prev1 + cw_ref[2:3, :] * a
        up = jnp.dot(h_scr[FFN_HALO:, :], wu_ref[...], preferred_element_type=F32)
        gated = (jax.nn.silu(conv) * up).astype(BF16)
        d_out = y_ref.shape[1]
        for n in range(d_out // n_chunk):
            sl = slice(n * n_chunk, (n + 1) * n_chunk)
            y_ref[:, sl] += jnp.dot(gated, wd_ref[:, sl], preferred_element_type=F32)
        a_ref[...] = a if has_state else a[tm - SUBLANES:]

    if has_tail:
        pl.when(f < n_main)(lambda: columns(groups[0], a_refs[0]))
        pl.when(f == n_main)(lambda: columns(groups[1], a_refs[1]))
    else:
        columns(groups[0], a_refs[0])


def _ffn(x1, g2, wg_b, wu_b, wd_b, conv_w, conv_b, seq_len, e1=None, e2=None, *, tm, tf):
    T, D = x1.shape
    F = wg_b.shape[1]
    has_state = e1 is not None
    assert seq_len % tm == 0 or tm % seq_len == 0
    n_main, tw = divmod(F, tf)
    has_tail = tw > 0
    assert n_main > 0 and (not has_tail or (n_main * tf) % tw == 0)
    blocks_per_halo = tm // FFN_HALO
    a_rows, a_blk = (T, tm) if has_state else (T // tm * SUBLANES, SUBLANES)

    def group(width, col, **mode):
        specs = [
            pl.BlockSpec((D, width), lambda i, f: (0, col(f)), **mode),
            pl.BlockSpec((D, width), lambda i, f: (0, col(f)), **mode),
            pl.BlockSpec((width, D), lambda i, f: (col(f), 0), **mode),
            pl.BlockSpec((CONV_W, width), lambda i, f: (0, col(f))),
            pl.BlockSpec((1, width), lambda i, f: (0, col(f))),
        ]
        ops = [wg_b, wu_b, wd_b, conv_w, conv_b]
        if has_state:
            specs += [pl.BlockSpec((tm, width), lambda i, f: (i, col(f)))] * 2
            ops += [e1, e2]
        return specs, ops

    in_specs = [
        pl.BlockSpec((tm, D), lambda i, f: (i, 0), pipeline_mode=pl.Buffered(1)),
        pl.BlockSpec((FFN_HALO, D), lambda i, f: (jnp.maximum(i * blocks_per_halo - 1, 0), 0)),
        pl.BlockSpec((1, D), lambda i, f: (0, 0)),
    ]
    args = [x1, x1, g2]
    main_col = lambda f: jnp.minimum(f, n_main - 1)
    specs, ops = group(tf, main_col)
    in_specs += specs
    args += ops
    out_specs = [pl.BlockSpec((tm, D), lambda i, f: (i, 0), pipeline_mode=pl.Buffered(1)),
                 pl.BlockSpec((a_blk, tf), lambda i, f: (i, main_col(f)))]
    out_shape = [jax.ShapeDtypeStruct((T, D), F32), jax.ShapeDtypeStruct((a_rows, n_main * tf), F32)]
    if has_tail:
        tail_col = n_main * tf // tw
        specs, ops = group(tw, lambda f: tail_col, pipeline_mode=pl.Buffered(1))
        in_specs += specs
        args += ops
        out_specs.append(pl.BlockSpec((a_blk, tw), lambda i, f: (i, 0)))
        out_shape.append(jax.ShapeDtypeStruct((a_rows, tw), F32))
    kern = functools.partial(_ffn_kernel, seq_len=seq_len, has_state=has_state, has_tail=has_tail,
                             n_main=n_main, n_chunk=min(D, 512))
    outs = pl.pallas_call(
        kern,
        grid=(T // tm, n_main + int(has_tail)),
        in_specs=in_specs,
        out_specs=out_specs,
        out_shape=out_shape,
        scratch_shapes=[pltpu.VMEM((FFN_HALO + tm, D), BF16)],
        compiler_params=_params("parallel", "arbitrary", vmem_limit_bytes=FFN_VMEM_LIMIT_BYTES),
        name="ffn",
    )(*args)
    a = jnp.concatenate(outs[1:], axis=1)
    return outs[0], a


def _layer(x3d, lw, *, layer=0, cache=None, ssm0=None, conv0=None):
    (norm1_g, w_in, q_norm_g, k_norm_g, a_re, a_im, log_dt, b_re, b_im, c_re, c_im, ssm_d,
     w_glu, b_glu, w_out, norm2_g, w_gate, w_up, conv_w, conv_b, w_down) = lw
    nb, L, D = x3d.shape
    T = nb * L
    x2d = x3d.reshape(T, D)
    d_ssm = ssm_d.shape[0]
    d_att = (w_in.shape[1] - d_ssm) // 3
    n_heads = d_att // HEAD_DIM
    n_groups = a_re.shape[0]
    tm = min(PROJ_TM, T)
    tm_ffn = min(FFN_TM, T)

    head_gain = jnp.concatenate([jnp.tile(q_norm_g.astype(F32), n_heads),
                                 jnp.tile(k_norm_g.astype(F32), n_heads)])[None]
    proj, k, v = _in_proj(x2d, norm1_g[None], w_in.astype(BF16), head_gain, n_heads,
                          tm=min(IN_PROJ_TM, T), tn=SUBLANES * HEAD_DIM)
    keep = min(MAX_WINDOW, L) if cache is None else L

    def head_major(kv):
        kv = kv.reshape(-1, nb, L, SUBLANES, HEAD_DIM)[:, :, L - keep:]
        return kv.transpose(1, 2, 0, 3, 4).reshape(nb, keep, n_heads, HEAD_DIM)

    k, v = head_major(k), head_major(v)

    b_blk, c_blk, lbr, lbi = _s5_pack(a_re, a_im, log_dt, b_re, b_im, c_re, c_im)
    d_skip = ssm_d.astype(F32)[None]
    if cache is None:
        o_att = _attn_prompt(proj, nb, L, n_heads)
        z, sre, sim = _s5_prompt(proj, nb, L, 3 * d_att // d_ssm, b_blk, c_blk, lbr, lbi, d_skip)
    else:
        o_att = _attn_sample(proj, cache[0], cache[1], layer, nb, n_heads).astype(BF16)
        u_tm = proj[:, 3 * d_att:].reshape(nb, L, d_ssm).transpose(1, 0, 2)
        z_tm, sre, sim = _s5_sample(u_tm, b_blk, c_blk, lbr, lbi, d_skip,
                                    ssm0[0].astype(F32).reshape(nb, n_groups * SSM_STATE),
                                    ssm0[1].astype(F32).reshape(nb, n_groups * SSM_STATE))
        z = z_tm.transpose(1, 0, 2).reshape(T, d_ssm)
    ssm_re = sre.reshape(nb, n_groups, SSM_STATE)
    ssm_im = sim.reshape(nb, n_groups, SSM_STATE)

    o_ssm = _glu(z, w_glu.astype(BF16), b_glu.astype(F32)[None], tm=tm)
    x1 = _out_proj(o_att, o_ssm, w_out.astype(BF16), x2d, tm=tm, tn=1024)

    ffn_w = (norm2_g[None], w_gate.astype(BF16), w_up.astype(BF16), w_down.astype(BF16),
             conv_w.astype(F32), conv_b.astype(F32)[None])
    F = w_gate.shape[1]
    if conv0 is None:
        y, a_tail = _ffn(x1, *ffn_w, L, tm=tm_ffn, tf=FFN_TF)
        tiles_per_seq = L // tm_ffn
        a_tail = a_tail.reshape(nb, tiles_per_seq, SUBLANES, F)
        conv_state = a_tail[:, -1, SUBLANES - (CONV_W - 1):]
    else:
        c0 = conv0.astype(F32)
        zeros = jnp.zeros((nb, L - 1, F), F32)
        e1 = jnp.concatenate([c0[:, 1:2], zeros], axis=1).reshape(T, F)
        e2 = jnp.concatenate([c0[:, 0:1], c0[:, 1:2], zeros[:, 1:]], axis=1).reshape(T, F)
        y, a_full = _ffn(x1, *ffn_w, L, e1, e2, tm=tm_ffn, tf=FFN_STATE_TF)
        conv_state = a_full.reshape(nb, L, F)[:, L - (CONV_W - 1):]
    return y.reshape(nb, L, D), k, v, ssm_re, ssm_im, conv_state


def kernel(x_prompt, x_sample, cache_k, cache_v, state_ssm_re, state_ssm_im, state_ffn_conv,
           norm1_g, w_in, q_norm_g, k_norm_g,
           ssm_a_re, ssm_a_im, ssm_log_dt, ssm_b_re, ssm_b_im, ssm_c_re, ssm_c_im,
           ssm_d, w_glu, b_glu, w_out, norm2_g,
           w_ffn_gate, w_ffn_up, ffn_conv_w, ffn_conv_b, w_ffn_down):
    depth = w_in.shape[0]
    outs_p = [[] for _ in range(5)]
    outs_s = [[] for _ in range(5)]
    yp, ys = x_prompt, x_sample
    for l in range(depth):
        lw = (norm1_g[l], w_in[l], q_norm_g[l], k_norm_g[l],
              ssm_a_re[l], ssm_a_im[l], ssm_log_dt[l], ssm_b_re[l], ssm_b_im[l], ssm_c_re[l], ssm_c_im[l],
              ssm_d[l], w_glu[l], b_glu[l], w_out[l], norm2_g[l],
              w_ffn_gate[l], w_ffn_up[l], ffn_conv_w[l], ffn_conv_b[l], w_ffn_down[l])
        yp, k_p, v_p, re_p, im_p, conv_p = _layer(yp, lw)
        ys, k_s, v_s, re_s, im_s, conv_s = _layer(
            ys, lw, layer=l, cache=(cache_k, cache_v),
            ssm0=(state_ssm_re[l], state_ssm_im[l]), conv0=state_ffn_conv[l])
        for lst, val in zip(outs_p, (k_p, v_p, re_p, im_p, conv_p)):
            lst.append(val)
        for lst, val in zip(outs_s, (k_s, v_s, re_s, im_s, conv_s)):
            lst.append(val)
    return (yp, ys, *(jnp.stack(o) for o in outs_p), *(jnp.stack(o) for o in outs_s))
```

```python
import functools
import math

import jax
import jax.numpy as jnp
import numpy as np
from jax import lax
from jax.experimental import pallas as pl
from jax.experimental.pallas import tpu as pltpu

F32 = jnp.float32
BF16 = jnp.bfloat16

HEAD_DIM = 128
DILATED_GROUPS = ((128, 1), (512, 4), (2048, 16))
MAX_WINDOW = max(w for w, _ in DILATED_GROUPS)
SSM_GROUP_CH = 16
SSM_STATE = 64
CONV_W = 3
NORM_EPS = 1e-6
MASKED = -1e30

LANES = 128
SUBLANES = 8
MXU_DIM = 256
VMEM_BYTES = 64 * 1024 * 1024
VMEM_LIMIT_BYTES = VMEM_BYTES - 4 * 1024 * 1024
FFN_VMEM_LIMIT_BYTES = VMEM_BYTES - 2 * 1024 * 1024

S5_GROUPS_PER_BLOCK = MXU_DIM // SSM_GROUP_CH
S5_BLOCK_CH = S5_GROUPS_PER_BLOCK * SSM_GROUP_CH
S5_BLOCK_STATE = S5_GROUPS_PER_BLOCK * SSM_STATE

ATTN_TQ = 256
ATTN_QBLOCKS = 4
S5_CHUNK = 256
PROJ_TM = 1024
IN_PROJ_TM = 512
FFN_TM = 512
FFN_STATE_TF = 256
FFN_TF = 512
FFN_HALO = 16


def _params(*sem, vmem_limit_bytes=VMEM_LIMIT_BYTES):
    return pltpu.CompilerParams(dimension_semantics=sem, vmem_limit_bytes=vmem_limit_bytes)


def _rms_rows(x, g):
    ms = jnp.mean(x * x, axis=-1, keepdims=True)
    return x * lax.rsqrt(ms + NORM_EPS) * g


def _rms_rows_to(dst_ref, dst_row0, src_ref, g_ref, chunk=128):
    rows = src_ref.shape[0]
    chunk = min(chunk, rows)

    def body(c, carry):
        r0 = pl.multiple_of(c * chunk, chunk)
        dst_ref[pl.ds(dst_row0 + r0, chunk), :] = _rms_rows(src_ref[pl.ds(r0, chunk), :], g_ref[...]).astype(BF16)
        return carry

    lax.fori_loop(0, rows // chunk, body, 0)


def _in_proj_kernel(x_ref, g_ref, w_ref, hg_ref, o_ref, ko_ref, vo_ref, h_scr, *, blocks_per_part):
    j = pl.program_id(1)
    nb = blocks_per_part
    heads = o_ref.shape[1] // HEAD_DIM
    tm = o_ref.shape[0]

    @pl.when(j == 0)
    def _():
        _rms_rows_to(h_scr, 0, x_ref, g_ref)

    acc = jnp.dot(h_scr[...], w_ref[...], preferred_element_type=F32)

    def head_normed(hh):
        sl = slice(hh * HEAD_DIM, (hh + 1) * HEAD_DIM)
        return sl, _rms_rows(acc[:, sl], hg_ref[:, sl])

    @pl.when(j < nb)
    def _():
        for hh in range(heads):
            sl, val = head_normed(hh)
            o_ref[:, sl] = val

    @pl.when(jnp.logical_and(j >= nb, j < 2 * nb))
    def _():
        for hh in range(heads):
            sl, val = head_normed(hh)
            o_ref[:, sl] = val
            ko_ref[pl.ds(hh, tm, stride=heads), :] = val

    @pl.when(jnp.logical_and(j >= 2 * nb, j < 3 * nb))
    def _():
        o_ref[...] = acc
        for hh in range(heads):
            vo_ref[pl.ds(hh, tm, stride=heads), :] = acc[:, hh * HEAD_DIM:(hh + 1) * HEAD_DIM]

    @pl.when(j >= 3 * nb)
    def _():
        o_ref[...] = acc


def _in_proj(x2d, g1, w_b, head_gain, n_heads, *, tm, tn):
    T, D = x2d.shape
    N = w_b.shape[1]
    assert tn == SUBLANES * HEAD_DIM and (n_heads * HEAD_DIM) % tn == 0
    nb = n_heads * HEAD_DIM // tn
    kern = functools.partial(_in_proj_kernel, blocks_per_part=nb)
    kv_shape = jax.ShapeDtypeStruct((nb, T * SUBLANES, HEAD_DIM), F32)
    return pl.pallas_call(
        kern,
        grid=(T // tm, N // tn),
        in_specs=[
            pl.BlockSpec((tm, D), lambda i, j: (i, 0)),
            pl.BlockSpec((1, D), lambda i, j: (0, 0)),
            pl.BlockSpec((D, tn), lambda i, j: (0, j)),
            pl.BlockSpec((1, tn), lambda i, j: (0, jnp.minimum(j, 2 * nb - 1))),
        ],
        out_specs=[
            pl.BlockSpec((tm, tn), lambda i, j: (i, j)),
            pl.BlockSpec((None, tm * SUBLANES, HEAD_DIM), lambda i, j: (jnp.clip(j - nb, 0, nb - 1), i, 0)),
            pl.BlockSpec((None, tm * SUBLANES, HEAD_DIM), lambda i, j: (jnp.clip(j - 2 * nb, 0, nb - 1), i, 0)),
        ],
        out_shape=[jax.ShapeDtypeStruct((T, N), F32), kv_shape, kv_shape],
        scratch_shapes=[pltpu.VMEM((tm, D), BF16)],
        compiler_params=_params("parallel", "arbitrary"),
        name="in_proj",
    )(x2d, g1, w_b, head_gain)


LOG2E = math.log2(math.e)


def _multiplicity(d):
    count = np.zeros(d.shape, np.int32)
    for window, dilation in DILATED_GROUPS:
        count += ((d >= 0) & (d <= window) & (d % dilation == 0)).astype(np.int32)
    return count


def _bias2(d, slope=0.0):
    count = _multiplicity(d)
    return np.where(count > 0, np.log2(np.maximum(count, 1)) - slope * LOG2E * d, MASKED).astype(np.float32)


def _alibi_slopes(n_heads):
    return (2.0 ** (-8.0 * np.arange(1, n_heads + 1, dtype=np.float64) / n_heads)).astype(np.float32)


def _attn_prompt_kernel(slopes_ref, q_ref, k_ref, v_ref, logc_ref, o_ref, kb_scr, vb_scr, bias_scr,
                        *, n_off):
    h = pl.program_id(1)
    qi = pl.program_id(2)
    tq = bias_scr.shape[0]
    pad = (n_off - 1) * tq

    @pl.when(qi == 0)
    def _():
        kb_scr[0:pad, :] = jnp.zeros((pad, HEAD_DIM), BF16)
        vb_scr[0:pad, :] = jnp.zeros((pad, HEAD_DIM), BF16)
        kb_scr[pad:, :] = k_ref[...].astype(BF16)
        vb_scr[pad:, :] = v_ref[...].astype(BF16)
        slope = slopes_ref[h]
        row = lax.broadcasted_iota(jnp.int32, (tq, tq), 0)
        col = lax.broadcasted_iota(jnp.int32, (tq, tq), 1)
        base = (row - col).astype(F32)
        for j in range(n_off):
            m = n_off - 1 - j
            bias_scr[:, j * tq:(j + 1) * tq] = logc_ref[m] - (slope * LOG2E) * (base + float(m * tq))

    n_sub = q_ref.shape[0] // tq

    def blocks(masked):
        for sub in range(n_sub):
            _attn_prompt_block(qi * n_sub + sub, q_ref.at[sub * tq:(sub + 1) * tq],
                               o_ref.at[sub * tq:(sub + 1) * tq], kb_scr, vb_scr, bias_scr,
                               n_off=n_off, masked=masked)

    pl.when(qi * n_sub >= n_off - 1)(lambda: blocks(False))
    pl.when(qi * n_sub < n_off - 1)(lambda: blocks(True))


def _attn_prompt_block(qb, q_ref, o_ref, kb_scr, vb_scr, bias_scr, *, n_off, masked):
    tq = q_ref.shape[0]
    wlen = n_off * tq
    q = (q_ref[...] * (HEAD_DIM ** -0.5 * LOG2E)).astype(BF16)
    start = pl.multiple_of(qb * tq, tq)
    kw = kb_scr[pl.ds(start, wlen), :]
    vw = vb_scr[pl.ds(start, wlen), :]
    s = lax.dot_general(q, kw, (((1,), (1,)), ((), ())), preferred_element_type=F32) + bias_scr[...]
    if masked:
        lane = lax.broadcasted_iota(jnp.int32, (1, wlen), 1)
        s = s + jnp.where(lane < (n_off - 1 - qb) * tq, MASKED, 0.0)
    m_i = jnp.max(s, axis=-1, keepdims=True)
    p = jnp.exp2(s - m_i)
    l_i = jnp.sum(p, axis=-1, keepdims=True)
    acc = jnp.dot(p.astype(BF16), vw, preferred_element_type=F32)
    o_ref[...] = (acc / l_i).astype(o_ref.dtype)


def _attn_prompt(proj, n_batch, seq, n_heads):
    tq = ATTN_TQ
    rows = ATTN_QBLOCKS * tq
    nq = seq // rows
    n_off = MAX_WINDOW // tq + 1
    m = np.arange(n_off)[:, None, None]
    r = np.arange(tq)[None, :, None]
    c = np.arange(tq)[None, None, :]
    logc = _bias2(m * tq + r - c)
    kern = functools.partial(_attn_prompt_kernel, n_off=n_off)
    pad = (n_off - 1) * tq
    return pl.pallas_call(
        kern,
        grid=(n_batch, n_heads, nq),
        in_specs=[
            pl.BlockSpec(memory_space=pltpu.SMEM),
            pl.BlockSpec((rows, HEAD_DIM), lambda b, h, i: (b * nq + i, h)),
            pl.BlockSpec((seq, HEAD_DIM), lambda b, h, i: (b, n_heads + h)),
            pl.BlockSpec((seq, HEAD_DIM), lambda b, h, i: (b, 2 * n_heads + h)),
            pl.BlockSpec((n_off, tq, tq), lambda b, h, i: (0, 0, 0)),
        ],
        out_specs=pl.BlockSpec((rows, HEAD_DIM), lambda b, h, i: (b * nq + i, h)),
        out_shape=jax.ShapeDtypeStruct((n_batch * seq, n_heads * HEAD_DIM), BF16),
        scratch_shapes=[pltpu.VMEM((pad + seq, HEAD_DIM), BF16), pltpu.VMEM((pad + seq, HEAD_DIM), BF16),
                        pltpu.VMEM((tq, n_off * tq), F32)],
        compiler_params=_params("parallel", "parallel", "arbitrary"),
        name="attn_prompt",
    )(_alibi_slopes(n_heads), proj, proj, proj, logc)


def _attn_sample_kernel(q_ref, kn_ref, vn_ref, kf_ref, kr_ref, vf_ref, vr_ref,
                        bf_ref, br_ref, bn_ref, o_ref, *, heads):
    n_new = q_ref.shape[0]
    nt = (((1,), (1,)), ((), ()))

    def by_head(ref):
        return jnp.concatenate([ref[:, h * HEAD_DIM:(h + 1) * HEAD_DIM] for h in range(heads)], axis=0)

    q = (by_head(q_ref) * (HEAD_DIM ** -0.5 * LOG2E)).astype(BF16)
    pad = jnp.zeros((LANES - heads * n_new, HEAD_DIM), F32)
    kn = jnp.concatenate([by_head(kn_ref), pad], axis=0).astype(BF16)
    vn = jnp.concatenate([by_head(vn_ref), pad], axis=0).astype(BF16)
    kf = kf_ref[...].reshape(-1, HEAD_DIM).astype(BF16)
    kr = kr_ref[...].reshape(-1, HEAD_DIM).astype(BF16)
    s_f = lax.dot_general(q, kf, nt, preferred_element_type=F32) + bf_ref[...]
    s_r = lax.dot_general(q, kr, nt, preferred_element_type=F32) + br_ref[...]
    s_n = lax.dot_general(q, kn, nt, preferred_element_type=F32) + bn_ref[...]
    m = jnp.maximum(jnp.maximum(jnp.max(s_f, axis=-1, keepdims=True), jnp.max(s_r, axis=-1, keepdims=True)),
                    jnp.max(s_n, axis=-1, keepdims=True))
    p_f = jnp.exp2(s_f - m)
    p_r = jnp.exp2(s_r - m)
    p_n = jnp.exp2(s_n - m)
    l = (jnp.sum(p_f, axis=-1, keepdims=True) + jnp.sum(p_r, axis=-1, keepdims=True)
         + jnp.sum(p_n, axis=-1, keepdims=True))
    vf = vf_ref[...].reshape(-1, HEAD_DIM).astype(BF16)
    vr = vr_ref[...].reshape(-1, HEAD_DIM).astype(BF16)
    o = (jnp.dot(p_f.astype(BF16), vf, preferred_element_type=F32)
         + jnp.dot(p_r.astype(BF16), vr, preferred_element_type=F32)
         + jnp.dot(p_n.astype(BF16), vn, preferred_element_type=F32)) / l
    for h in range(heads):
        o_ref[:, h * HEAD_DIM:(h + 1) * HEAD_DIM] = o[h * n_new:(h + 1) * n_new]


def _attn_sample(proj, cache_k, cache_v, layer, n_seq, n_heads):
    depth, _, win, _, _ = cache_k.shape
    n_new = proj.shape[0] // n_seq
    hs = SUBLANES
    nhg = n_heads // hs
    (w_far, dil), (w_mid, _) = sorted(DILATED_GROUPS, reverse=True)[:2]
    assert n_new == SUBLANES and win % dil == 0 and (win - w_mid) % dil == 0 and win >= w_far >= w_mid
    n_chunk = win // dil
    far_chunks = (win - w_mid) // dil
    rec_chunks = n_chunk - far_chunks
    assert far_chunks % rec_chunks == 0
    kc = cache_k.reshape(depth * n_seq, n_chunk, dil, n_heads, HEAD_DIM)
    vc = cache_v.reshape(depth * n_seq, n_chunk, dil, n_heads, HEAD_DIM)

    slopes = _alibi_slopes(n_heads).astype(np.float64)
    same_head = np.arange(hs)[:, None, None, None, None] == np.arange(hs)[None, None, None, None, :]
    i = np.arange(n_new)[None, :, None, None, None]

    def table(chunks, rows):
        r = (chunks[:, None] * dil + rows[None, :])[None, None, :, :, None]
        d = np.broadcast_to(win + i - r, (hs, n_new, len(chunks), len(rows), hs))
        out = [np.where(same_head, _bias2(d, slopes[g * hs:(g + 1) * hs, None, None, None, None]), MASKED)
               for g in range(nhg)]
        return np.stack(out).reshape(nhg, hs * n_new, -1).astype(np.float32)

    chunks = np.arange(n_chunk)
    rows = np.arange(dil)
    bias_f = table(chunks[:far_chunks], rows[:n_new])
    bias_r = table(chunks[far_chunks:], rows)
    skipped_rows = (chunks[:far_chunks, None] * dil + rows[None, n_new:]).reshape(-1)
    assert not _multiplicity(win + np.arange(n_new)[:, None] - skipped_rows[None, :]).any()
    j = np.arange(LANES)
    hj, tj = j // n_new, j % n_new
    d_n = np.broadcast_to(np.arange(n_new)[None, :, None] - tj[None, None, :], (hs, n_new, LANES))
    own = hj[None, None, :] == np.arange(hs)[:, None, None]
    bias_n = np.stack([np.where(own, _bias2(d_n, slopes[g * hs:(g + 1) * hs, None, None]), MASKED)
                       for g in range(nhg)]).reshape(nhg, hs * n_new, LANES).astype(np.float32)

    wb = hs * HEAD_DIM
    q_blocks = n_heads * HEAD_DIM // wb
    base = layer * n_seq
    kern = functools.partial(_attn_sample_kernel, heads=hs)
    far_spec = pl.BlockSpec((None, far_chunks, n_new, hs, HEAD_DIM), lambda g, b: (base + b, 0, 0, g, 0))
    rec_spec = pl.BlockSpec((None, rec_chunks, dil, hs, HEAD_DIM),
                            lambda g, b: (base + b, far_chunks // rec_chunks, 0, g, 0))
    return pl.pallas_call(
        kern,
        grid=(nhg, n_seq),
        in_specs=[
            pl.BlockSpec((n_new, wb), lambda g, b: (b, g)),
            pl.BlockSpec((n_new, wb), lambda g, b: (b, q_blocks + g)),
            pl.BlockSpec((n_new, wb), lambda g, b: (b, 2 * q_blocks + g)),
            far_spec, rec_spec, far_spec, rec_spec,
            pl.BlockSpec((None,) + bias_f.shape[1:], lambda g, b: (g, 0, 0)),
            pl.BlockSpec((None,) + bias_r.shape[1:], lambda g, b: (g, 0, 0)),
            pl.BlockSpec((None,) + bias_n.shape[1:], lambda g, b: (g, 0, 0)),
        ],
        out_specs=pl.BlockSpec((n_new, wb), lambda g, b: (b, g)),
        out_shape=jax.ShapeDtypeStruct((n_seq * n_new, n_heads * HEAD_DIM), F32),
        compiler_params=_params("parallel", "parallel"),
        name="attn_sample",
    )(proj, proj, proj, kc, kc, vc, vc, bias_f, bias_r, bias_n)


def _s5_pack(a_re, a_im, log_dt, b_re, b_im, c_re, c_im):
    dt = jnp.exp(log_dt.astype(F32))[:, None]
    ar, ai = a_re.astype(F32), a_im.astype(F32)
    mag = jnp.exp(ar * dt)
    lb_re, lb_im = mag * jnp.cos(ai * dt), mag * jnp.sin(ai * dt)
    den = ar * ar + ai * ai
    ir, ii = ar / den, -ai / den
    cr = (lb_re - 1.0) * ir - lb_im * ii
    ci = (lb_re - 1.0) * ii + lb_im * ir
    bb_re = cr[..., None] * b_re - ci[..., None] * b_im
    bb_im = cr[..., None] * b_im + ci[..., None] * b_re
    n_blk = a_re.shape[0] // S5_GROUPS_PER_BLOCK
    gb, p, ch = S5_GROUPS_PER_BLOCK, SSM_STATE, SSM_GROUP_CH
    eye = jnp.eye(gb, dtype=F32)

    def pack_in(bb):
        return jnp.einsum('aGpc,GH->aGcHp', bb.reshape(n_blk, gb, p, ch), eye).reshape(n_blk, gb * ch, gb * p)

    def pack_out(cc):
        return jnp.einsum('aGcp,GH->aHpGc', cc.reshape(n_blk, gb, ch, p), eye).reshape(n_blk, gb * p, gb * ch)

    b_blk = jnp.concatenate([pack_in(bb_re), pack_in(bb_im)], axis=2).astype(BF16)
    c_blk = jnp.concatenate([pack_out(c_re.astype(F32)), -pack_out(c_im.astype(F32))], axis=1).astype(BF16)
    lbr = lb_re.reshape(n_blk, gb * p)
    lbi = lb_im.reshape(n_blk, gb * p)
    return b_blk, c_blk, lbr, lbi


def _s5_prompt_kernel(u_ref, b_ref, c_ref, lbr_ref, lbi_ref, d_ref, z_ref, sre_ref, sim_ref,
                      s_scr, xr_scr, xi_scr):
    i = pl.program_id(1)
    lc = u_ref.shape[0]
    n_blk = b_ref.shape[0]
    n_slab = s_scr.shape[0]
    half = n_slab // 2

    @pl.when(i == 0)
    def _():
        xr_scr[...] = jnp.zeros_like(xr_scr)
        xi_scr[...] = jnp.zeros_like(xi_scr)

    u = u_ref[...]
    ub = u.astype(BF16)
    for g in range(n_blk):
        bu = jnp.dot(ub[:, g * S5_BLOCK_CH:(g + 1) * S5_BLOCK_CH], b_ref[g], preferred_element_type=F32)
        for s in range(n_slab):
            s_scr[s, pl.ds(g, lc, stride=n_blk), :] = bu[:, s * LANES:(s + 1) * LANES]

    def step(t, carry):
        row = pl.multiple_of(t * n_blk, n_blk)
        new = []
        for s in range(half):
            xr, xi = carry[s], carry[half + s]
            lr = lbr_ref[:, s * LANES:(s + 1) * LANES]
            li = lbi_ref[:, s * LANES:(s + 1) * LANES]
            nr = lr * xr - li * xi + s_scr[s, pl.ds(row, n_blk), :]
            ni = lr * xi + li * xr + s_scr[half + s, pl.ds(row, n_blk), :]
            s_scr[s, pl.ds(row, n_blk), :] = nr
            s_scr[half + s, pl.ds(row, n_blk), :] = ni
            new.append((nr, ni))
        return tuple(n[0] for n in new) + tuple(n[1] for n in new)

    init = tuple(xr_scr[:, s * LANES:(s + 1) * LANES] for s in range(half)) + \
        tuple(xi_scr[:, s * LANES:(s + 1) * LANES] for s in range(half))
    fin = lax.fori_loop(0, lc, step, init, unroll=4)
    for s in range(half):
        xr_scr[:, s * LANES:(s + 1) * LANES] = fin[s]
        xi_scr[:, s * LANES:(s + 1) * LANES] = fin[half + s]
    sre_ref[...] = xr_scr[...]
    sim_ref[...] = xi_scr[...]

    for g in range(n_blk):
        xs = jnp.concatenate([s_scr[s, pl.ds(g, lc, stride=n_blk), :] for s in range(n_slab)], axis=1)
        y = jnp.dot(xs.astype(BF16), c_ref[g], preferred_element_type=F32)
        sl = slice(g * S5_BLOCK_CH, (g + 1) * S5_BLOCK_CH)
        y = y + d_ref[:, sl] * u[:, sl]
        z_ref[:, sl] = jax.nn.gelu(y)


def _s5_prompt(proj, n_batch, seq, u_col_block, b_blk, c_blk, lbr, lbi, d_skip, *, lc=S5_CHUNK):
    n_blk = b_blk.shape[0]
    assert n_blk == SUBLANES
    d_ssm = n_blk * S5_BLOCK_CH
    nst = S5_BLOCK_STATE
    nc = seq // lc
    n_slab = 2 * nst // LANES
    z, sre, sim = pl.pallas_call(
        _s5_prompt_kernel,
        grid=(n_batch, nc),
        in_specs=[
            pl.BlockSpec((lc, d_ssm), lambda b, i: (b * nc + i, u_col_block)),
            pl.BlockSpec(b_blk.shape, lambda b, i: (0, 0, 0), pipeline_mode=pl.Buffered(1)),
            pl.BlockSpec(c_blk.shape, lambda b, i: (0, 0, 0), pipeline_mode=pl.Buffered(1)),
            pl.BlockSpec(lbr.shape, lambda b, i: (0, 0)),
            pl.BlockSpec(lbi.shape, lambda b, i: (0, 0)),
            pl.BlockSpec((1, d_ssm), lambda b, i: (0, 0)),
        ],
        out_specs=[
            pl.BlockSpec((lc, d_ssm), lambda b, i: (b * nc + i, 0)),
            pl.BlockSpec((None, n_blk, nst), lambda b, i: (b, 0, 0)),
            pl.BlockSpec((None, n_blk, nst), lambda b, i: (b, 0, 0)),
        ],
        out_shape=[
            jax.ShapeDtypeStruct((n_batch * seq, d_ssm), F32),
            jax.ShapeDtypeStruct((n_batch, n_blk, nst), F32),
            jax.ShapeDtypeStruct((n_batch, n_blk, nst), F32),
        ],
        scratch_shapes=[pltpu.VMEM((n_slab, lc * n_blk, LANES), F32),
                        pltpu.VMEM((n_blk, nst), F32), pltpu.VMEM((n_blk, nst), F32)],
        compiler_params=_params("parallel", "arbitrary"),
        name="s5_prompt",
    )(proj, b_blk, c_blk, lbr, lbi, d_skip)
    return z, sre, sim


def _s5_sample_kernel(u_ref, b_ref, c_ref, lbr_ref, lbi_ref, d_ref, x0r_ref, x0i_ref,
                      z_ref, xr_ref, xi_ref):
    n_steps = u_ref.shape[0]
    lr = lbr_ref[...]
    li = lbi_ref[...]
    xr = x0r_ref[...]
    xi = x0i_ref[...]
    nst = xr.shape[1]
    for t in range(n_steps):
        u = u_ref[t]
        bu = jnp.dot(u.astype(BF16), b_ref[...], preferred_element_type=F32)
        xr, xi = lr * xr - li * xi + bu[:, :nst], lr * xi + li * xr + bu[:, nst:]
        xs = jnp.concatenate([xr, xi], axis=1).astype(BF16)
        y = jnp.dot(xs, c_ref[...], preferred_element_type=F32) + d_ref[...] * u
        z_ref[t] = jax.nn.gelu(y)
    xr_ref[...] = xr
    xi_ref[...] = xi


def _s5_sample(u_tm, b_blk, c_blk, lbr, lbi, d_skip, x0r, x0i):
    n_steps, n_seq, d_ssm = u_tm.shape
    n_blk = b_blk.shape[0]
    nst = S5_BLOCK_STATE
    bc = S5_BLOCK_CH
    return pl.pallas_call(
        _s5_sample_kernel,
        grid=(n_blk,),
        in_specs=[
            pl.BlockSpec((n_steps, n_seq, bc), lambda g: (0, 0, g)),
            pl.BlockSpec((None, bc, 2 * nst), lambda g: (g, 0, 0)),
            pl.BlockSpec((None, 2 * nst, bc), lambda g: (g, 0, 0)),
            pl.BlockSpec((None, 1, nst), lambda g: (g, 0, 0)),
            pl.BlockSpec((None, 1, nst), lambda g: (g, 0, 0)),
            pl.BlockSpec((1, bc), lambda g: (0, g)),
            pl.BlockSpec((n_seq, nst), lambda g: (0, g)),
            pl.BlockSpec((n_seq, nst), lambda g: (0, g)),
        ],
        out_specs=[
            pl.BlockSpec((n_steps, n_seq, bc), lambda g: (0, 0, g)),
            pl.BlockSpec((n_seq, nst), lambda g: (0, g)),
            pl.BlockSpec((n_seq, nst), lambda g: (0, g)),
        ],
        out_shape=[
            jax.ShapeDtypeStruct((n_steps, n_seq, d_ssm), F32),
            jax.ShapeDtypeStruct(x0r.shape, F32),
            jax.ShapeDtypeStruct(x0i.shape, F32),
        ],
        compiler_params=_params("parallel"),
        name="s5_sample",
    )(u_tm, b_blk, c_blk, lbr.reshape(n_blk, 1, nst), lbi.reshape(n_blk, 1, nst), d_skip, x0r, x0i)


def _glu_kernel(z_ref, w_ref, b_ref, o_ref):
    z = z_ref[...]
    gate = jnp.dot(z.astype(BF16), w_ref[...], preferred_element_type=F32) + b_ref[...]
    o_ref[...] = (z * jax.nn.sigmoid(gate)).astype(o_ref.dtype)


def _glu(z, w_b, b, *, tm):
    T, N = z.shape
    return pl.pallas_call(
        _glu_kernel,
        grid=(T // tm,),
        in_specs=[
            pl.BlockSpec((tm, N), lambda i: (i, 0)),
            pl.BlockSpec((N, N), lambda i: (0, 0), pipeline_mode=pl.Buffered(1)),
            pl.BlockSpec((1, N), lambda i: (0, 0)),
        ],
        out_specs=pl.BlockSpec((tm, N), lambda i: (i, 0)),
        out_shape=jax.ShapeDtypeStruct((T, N), BF16),
        compiler_params=_params("parallel"),
        name="glu",
    )(z, w_b, b)


def _out_proj_kernel(a1_ref, a2_ref, w1_ref, w2_ref, x_ref, o_ref):
    acc = jnp.dot(a1_ref[...], w1_ref[...], preferred_element_type=F32)
    acc = acc + jnp.dot(a2_ref[...], w2_ref[...], preferred_element_type=F32)
    o_ref[...] = x_ref[...] + acc


def _out_proj(o_att, o_ssm, w_b, x2d, *, tm, tn):
    T, D = x2d.shape
    k1, k2 = o_att.shape[1], o_ssm.shape[1]
    assert k1 == k2
    return pl.pallas_call(
        _out_proj_kernel,
        grid=(T // tm, D // tn),
        in_specs=[
            pl.BlockSpec((tm, k1), lambda i, j: (i, 0)),
            pl.BlockSpec((tm, k2), lambda i, j: (i, 0)),
            pl.BlockSpec((k1, tn), lambda i, j: (0, j)),
            pl.BlockSpec((k2, tn), lambda i, j: (1, j)),
            pl.BlockSpec((tm, tn), lambda i, j: (i, j)),
        ],
        out_specs=pl.BlockSpec((tm, tn), lambda i, j: (i, j)),
        out_shape=jax.ShapeDtypeStruct((T, D), F32),
        compiler_params=_params("parallel", "arbitrary"),
        name="out_proj",
    )(o_att, o_ssm, w_b, w_b, x2d)


def _ffn_kernel(*refs, seq_len, has_state, has_tail, n_main, n_chunk):
    n_grp = 7 if has_state else 5
    x_ref, halo_ref, g_ref = refs[:3]
    groups = [refs[3:3 + n_grp]]
    n_out = 2
    if has_tail:
        groups.append(refs[3 + n_grp:3 + 2 * n_grp])
        n_out = 3
    outs = refs[3 + len(groups) * n_grp:]
    y_ref, a_refs, h_scr = outs[0], outs[1:n_out], outs[n_out]
    i = pl.program_id(0)
    f = pl.program_id(1)
    tm = x_ref.shape[0]

    @pl.when(f == 0)
    def _():
        _rms_rows_to(h_scr, 0, halo_ref, g_ref)
        _rms_rows_to(h_scr, FFN_HALO, x_ref, g_ref)
        y_ref[...] = x_ref[...]

    def columns(group, a_ref):
        wg_ref, wu_ref, wd_ref, cw_ref, cb_ref = group[:5]
        tf = wg_ref.shape[1]
        a_ext = jnp.dot(h_scr[...], wg_ref[...], preferred_element_type=F32)
        a = a_ext[FFN_HALO:]
        pos = (i * tm + lax.broadcasted_iota(jnp.int32, (tm, tf), 0)) % seq_len
        prev1 = a_ext[FFN_HALO - 1:FFN_HALO - 1 + tm]
        prev2 = a_ext[FFN_HALO - 2:FFN_HALO - 2 + tm]
        if has_state:
            prev1 = jnp.where(pos >= 1, prev1, group[5][...])
            prev2 = jnp.where(pos >= 2, prev2, group[6][...])
        else:
            prev1 = jnp.where(pos >= 1, prev1, 0.0)
            prev2 = jnp.where(pos >= 2, prev2, 0.0)
        conv = cb_ref[...] + cw_ref[0:1, :] * prev2 + cw_ref[1:2, :] * prev1 + cw_ref[2:3, :] * a
        up = jnp.dot(h_scr[FFN_HALO:, :], wu_ref[...], preferred_element_type=F32)
        gated = (jax.nn.silu(conv) * up).astype(BF16)
        d_out = y_ref.shape[1]
        for n in range(d_out // n_chunk):
            sl = slice(n * n_chunk, (n + 1) * n_chunk)
            y_ref[:, sl] += jnp.dot(gated, wd_ref[:, sl], preferred_element_type=F32)
        a_ref[...] = a if has_state else a[tm - SUBLANES:]

    if has_tail:
        pl.when(f < n_main)(lambda: columns(groups[0], a_refs[0]))
        pl.when(f == n_main)(lambda: columns(groups[1], a_refs[1]))
    else:
        columns(groups[0], a_refs[0])


def _ffn(x1, g2, wg_b, wu_b, wd_b, conv_w, conv_b, seq_len, e1=None, e2=None, *, tm, tf):
    T, D = x1.shape
    F = wg_b.shape[1]
    has_state = e1 is not None
    assert seq_len % tm == 0 or tm % seq_len == 0
    n_main, tw = divmod(F, tf)
    has_tail = tw > 0
    assert n_main > 0 and (not has_tail or (n_main * tf) % tw == 0)
    blocks_per_halo = tm // FFN_HALO
    a_rows, a_blk = (T, tm) if has_state else (T // tm * SUBLANES, SUBLANES)

    def group(width, col, **mode):
        specs = [
            pl.BlockSpec((D, width), lambda i, f: (0, col(f)), **mode),
            pl.BlockSpec((D, width), lambda i, f: (0, col(f)), **mode),
            pl.BlockSpec((width, D), lambda i, f: (col(f), 0), **mode),
            pl.BlockSpec((CONV_W, width), lambda i, f: (0, col(f))),
            pl.BlockSpec((1, width), lambda i, f: (0, col(f))),
        ]
        ops = [wg_b, wu_b, wd_b, conv_w, conv_b]
        if has_state:
            specs += [pl.BlockSpec((tm, width), lambda i, f: (i, col(f)))] * 2
            ops += [e1, e2]
        return specs, ops

    in_specs = [
        pl.BlockSpec((tm, D), lambda i, f: (i, 0), pipeline_mode=pl.Buffered(1)),
        pl.BlockSpec((FFN_HALO, D), lambda i, f: (jnp.maximum(i * blocks_per_halo - 1, 0), 0)),
        pl.BlockSpec((1, D), lambda i, f: (0, 0)),
    ]
    args = [x1, x1, g2]
    main_col = lambda f: jnp.minimum(f, n_main - 1)
    specs, ops = group(tf, main_col)
    in_specs += specs
    args += ops
    out_specs = [pl.BlockSpec((tm, D), lambda i, f: (i, 0), pipeline_mode=pl.Buffered(1)),
                 pl.BlockSpec((a_blk, tf), lambda i, f: (i, main_col(f)))]
    out_shape = [jax.ShapeDtypeStruct((T, D), F32), jax.ShapeDtypeStruct((a_rows, n_main * tf), F32)]
    if has_tail:
        tail_col = n_main * tf // tw
        specs, ops = group(tw, lambda f: tail_col, pipeline_mode=pl.Buffered(1))
        in_specs += specs
        args += ops
        out_specs.append(pl.BlockSpec((a_blk, tw), lambda i, f: (i, 0)))
        out_shape.append(jax.ShapeDtypeStruct((a_rows, tw), F32))
    kern = functools.partial(_ffn_kernel, seq_len=seq_len, has_state=has_state, has_tail=has_tail,
                             n_main=n_main, n_chunk=min(D, 512))
    outs = pl.pallas_call(
        kern,
        grid=(T // tm, n_main + int(has_tail)),
        in_specs=in_specs,
        out_specs=out_specs,
        out_shape=out_shape,
        scratch_shapes=[pltpu.VMEM((FFN_HALO + tm, D), BF16)],
        compiler_params=_params("parallel", "arbitrary", vmem_limit_bytes=FFN_VMEM_LIMIT_BYTES),
        name="ffn",
    )(*args)
    a = jnp.concatenate(outs[1:], axis=1)
    return outs[0], a


def _layer(x3d, lw, *, layer=0, cache=None, ssm0=None, conv0=None):
    (norm1_g, w_in, q_norm_g, k_norm_g, a_re, a_im, log_dt, b_re, b_im, c_re, c_im, ssm_d,
     w_glu, b_glu, w_out, norm2_g, w_gate, w_up, conv_w, conv_b, w_down) = lw
    nb, L, D = x3d.shape
    T = nb * L
    x2d = x3d.reshape(T, D)
    d_ssm = ssm_d.shape[0]
    d_att = (w_in.shape[1] - d_ssm) // 3
    n_heads = d_att // HEAD_DIM
    n_groups = a_re.shape[0]
    tm = min(PROJ_TM, T)
    tm_ffn = min(FFN_TM, T)

    head_gain = jnp.concatenate([jnp.tile(q_norm_g.astype(F32), n_heads),
                                 jnp.tile(k_norm_g.astype(F32), n_heads)])[None]
    proj, k, v = _in_proj(x2d, norm1_g[None], w_in.astype(BF16), head_gain, n_heads,
                          tm=min(IN_PROJ_TM, T), tn=SUBLANES * HEAD_DIM)
    keep = min(MAX_WINDOW, L) if cache is None else L

    def head_major(kv):
        kv = kv.reshape(-1, nb, L, SUBLANES, HEAD_DIM)[:, :, L - keep:]
        return kv.transpose(1, 2, 0, 3, 4).reshape(nb, keep, n_heads, HEAD_DIM)

    k, v = head_major(k), head_major(v)

    b_blk, c_blk, lbr, lbi = _s5_pack(a_re, a_im, log_dt, b_re, b_im, c_re, c_im)
    d_skip = ssm_d.astype(F32)[None]
    if cache is None:
        o_att = _attn_prompt(proj, nb, L, n_heads)
        z, sre, sim = _s5_prompt(proj, nb, L, 3 * d_att // d_ssm, b_blk, c_blk, lbr, lbi, d_skip)
    else:
        o_att = _attn_sample(proj, cache[0], cache[1], layer, nb, n_heads).astype(BF16)
        u_tm = proj[:, 3 * d_att:].reshape(nb, L, d_ssm).transpose(1, 0, 2)
        z_tm, sre, sim = _s5_sample(u_tm, b_blk, c_blk, lbr, lbi, d_skip,
                                    ssm0[0].astype(F32).reshape(nb, n_groups * SSM_STATE),
                                    ssm0[1].astype(F32).reshape(nb, n_groups * SSM_STATE))
        z = z_tm.transpose(1, 0, 2).reshape(T, d_ssm)
    ssm_re = sre.reshape(nb, n_groups, SSM_STATE)
    ssm_im = sim.reshape(nb, n_groups, SSM_STATE)

    o_ssm = _glu(z, w_glu.astype(BF16), b_glu.astype(F32)[None], tm=tm)
    x1 = _out_proj(o_att, o_ssm, w_out.astype(BF16), x2d, tm=tm, tn=1024)

    ffn_w = (norm2_g[None], w_gate.astype(BF16), w_up.astype(BF16), w_down.astype(BF16),
             conv_w.astype(F32), conv_b.astype(F32)[None])
    F = w_gate.shape[1]
    if conv0 is None:
        y, a_tail = _ffn(x1, *ffn_w, L, tm=tm_ffn, tf=FFN_TF)
        tiles_per_seq = L // tm_ffn
        a_tail = a_tail.reshape(nb, tiles_per_seq, SUBLANES, F)
        conv_state = a_tail[:, -1, SUBLANES - (CONV_W - 1):]
    else:
        c0 = conv0.astype(F32)
        zeros = jnp.zeros((nb, L - 1, F), F32)
        e1 = jnp.concatenate([c0[:, 1:2], zeros], axis=1).reshape(T, F)
        e2 = jnp.concatenate([c0[:, 0:1], c0[:, 1:2], zeros[:, 1:]], axis=1).reshape(T, F)
        y, a_full = _ffn(x1, *ffn_w, L, e1, e2, tm=tm_ffn, tf=FFN_STATE_TF)
        conv_state = a_full.reshape(nb, L, F)[:, L - (CONV_W - 1):]
    return y.reshape(nb, L, D), k, v, ssm_re, ssm_im, conv_state


def kernel(x_prompt, x_sample, cache_k, cache_v, state_ssm_re, state_ssm_im, state_ffn_conv,
           norm1_g, w_in, q_norm_g, k_norm_g,
           ssm_a_re, ssm_a_im, ssm_log_dt, ssm_b_re, ssm_b_im, ssm_c_re, ssm_c_im,
           ssm_d, w_glu, b_glu, w_out, norm2_g,
           w_ffn_gate, w_ffn_up, ffn_conv_w, ffn_conv_b, w_ffn_down):
    depth = w_in.shape[0]
    outs_p = [[] for _ in range(5)]
    outs_s = [[] for _ in range(5)]
    yp, ys = x_prompt, x_sample
    for l in range(depth):
        lw = (norm1_g[l], w_in[l], q_norm_g[l], k_norm_g[l],
              ssm_a_re[l], ssm_a_im[l], ssm_log_dt[l], ssm_b_re[l], ssm_b_im[l], ssm_c_re[l], ssm_c_im[l],
              ssm_d[l], w_glu[l], b_glu[l], w_out[l], norm2_g[l],
              w_ffn_gate[l], w_ffn_up[l], ffn_conv_w[l], ffn_conv_b[l], w_ffn_down[l])
        yp, k_p, v_p, re_p, im_p, conv_p = _layer(yp, lw)
        ys, k_s, v_s, re_s, im_s, conv_s = _layer(
            ys, lw, layer=l, cache=(cache_k, cache_v),
            ssm0=(state_ssm_re[l], state_ssm_im[l]), conv0=state_ffn_conv[l])
        for lst, val in zip(outs_p, (k_p, v_p, re_p, im_p, conv_p)):
            lst.append(val)
        for lst, val in zip(outs_s, (k_s, v_s, re_s, im_s, conv_s)):
            lst.append(val)
    return (yp, ys, *(jnp.stack(o) for o in outs_p), *(jnp.stack(o) for o in outs_s))
```

```python
import functools
import math

import jax
import jax.numpy as jnp
import numpy as np
from jax import lax
from jax.experimental import pallas as pl
from jax.experimental.pallas import tpu as pltpu

F32 = jnp.float32
BF16 = jnp.bfloat16

HEAD_DIM = 128
DILATED_GROUPS = ((128, 1), (512, 4), (2048, 16))
MAX_WINDOW = max(w for w, _ in DILATED_GROUPS)
SSM_GROUP_CH = 16
SSM_STATE = 64
CONV_W = 3
NORM_EPS = 1e-6
MASKED = -1e30

LANES = 128
SUBLANES = 8
MXU_DIM = 256
VMEM_BYTES = 64 * 1024 * 1024
VMEM_LIMIT_BYTES = VMEM_BYTES - 4 * 1024 * 1024
FFN_VMEM_LIMIT_BYTES = VMEM_BYTES - 2 * 1024 * 1024

S5_GROUPS_PER_BLOCK = MXU_DIM // SSM_GROUP_CH
S5_BLOCK_CH = S5_GROUPS_PER_BLOCK * SSM_GROUP_CH
S5_BLOCK_STATE = S5_GROUPS_PER_BLOCK * SSM_STATE

ATTN_TQ = 256
ATTN_QBLOCKS = 8
S5_CHUNK = 256
PROJ_TM = 1024
IN_PROJ_TM = 512
FFN_TM = 512
FFN_STATE_TF = 256
FFN_TF = 512
FFN_HALO = 16


def _params(*sem, vmem_limit_bytes=VMEM_LIMIT_BYTES):
    return pltpu.CompilerParams(dimension_semantics=sem, vmem_limit_bytes=vmem_limit_bytes)


def _rms_rows(x, g):
    ms = jnp.mean(x * x, axis=-1, keepdims=True)
    return x * lax.rsqrt(ms + NORM_EPS) * g


def _rms_rows_to(dst_ref, dst_row0, src_ref, g_ref, chunk=128):
    rows = src_ref.shape[0]
    chunk = min(chunk, rows)

    def body(c, carry):
        r0 = pl.multiple_of(c * chunk, chunk)
        dst_ref[pl.ds(dst_row0 + r0, chunk), :] = _rms_rows(src_ref[pl.ds(r0, chunk), :], g_ref[...]).astype(BF16)
        return carry

    lax.fori_loop(0, rows // chunk, body, 0)


def _in_proj_kernel(x_ref, g_ref, w_ref, hg_ref, o_ref, ko_ref, vo_ref, h_scr, *, blocks_per_part):
    j = pl.program_id(1)
    nb = blocks_per_part
    heads = o_ref.shape[1] // HEAD_DIM
    tm = o_ref.shape[0]

    @pl.when(j == 0)
    def _():
        _rms_rows_to(h_scr, 0, x_ref, g_ref)

    acc = jnp.dot(h_scr[...], w_ref[...], preferred_element_type=F32)

    def head_normed(hh):
        sl = slice(hh * HEAD_DIM, (hh + 1) * HEAD_DIM)
        return sl, _rms_rows(acc[:, sl], hg_ref[:, sl])

    @pl.when(j < nb)
    def _():
        for hh in range(heads):
            sl, val = head_normed(hh)
            o_ref[:, sl] = val

    @pl.when(jnp.logical_and(j >= nb, j < 2 * nb))
    def _():
        for hh in range(heads):
            sl, val = head_normed(hh)
            o_ref[:, sl] = val
            ko_ref[pl.ds(hh, tm, stride=heads), :] = val

    @pl.when(jnp.logical_and(j >= 2 * nb, j < 3 * nb))
    def _():
        o_ref[...] = acc
        for hh in range(heads):
            vo_ref[pl.ds(hh, tm, stride=heads), :] = acc[:, hh * HEAD_DIM:(hh + 1) * HEAD_DIM]

    @pl.when(j >= 3 * nb)
    def _():
        o_ref[...] = acc


def _in_proj(x2d, g1, w_b, head_gain, n_heads, *, tm, tn):
    T, D = x2d.shape
    N = w_b.shape[1]
    assert tn == SUBLANES * HEAD_DIM and (n_heads * HEAD_DIM) % tn == 0
    nb = n_heads * HEAD_DIM // tn
    kern = functools.partial(_in_proj_kernel, blocks_per_part=nb)
    kv_shape = jax.ShapeDtypeStruct((nb, T * SUBLANES, HEAD_DIM), F32)
    return pl.pallas_call(
        kern,
        grid=(T // tm, N // tn),
        in_specs=[
            pl.BlockSpec((tm, D), lambda i, j: (i, 0)),
            pl.BlockSpec((1, D), lambda i, j: (0, 0)),
            pl.BlockSpec((D, tn), lambda i, j: (0, j)),
            pl.BlockSpec((1, tn), lambda i, j: (0, jnp.minimum(j, 2 * nb - 1))),
        ],
        out_specs=[
            pl.BlockSpec((tm, tn), lambda i, j: (i, j)),
            pl.BlockSpec((None, tm * SUBLANES, HEAD_DIM), lambda i, j: (jnp.clip(j - nb, 0, nb - 1), i, 0)),
            pl.BlockSpec((None, tm * SUBLANES, HEAD_DIM), lambda i, j: (jnp.clip(j - 2 * nb, 0, nb - 1), i, 0)),
        ],
        out_shape=[jax.ShapeDtypeStruct((T, N), F32), kv_shape, kv_shape],
        scratch_shapes=[pltpu.VMEM((tm, D), BF16)],
        compiler_params=_params("parallel", "arbitrary"),
        name="in_proj",
    )(x2d, g1, w_b, head_gain)


LOG2E = math.log2(math.e)


def _multiplicity(d):
    count = np.zeros(d.shape, np.int32)
    for window, dilation in DILATED_GROUPS:
        count += ((d >= 0) & (d <= window) & (d % dilation == 0)).astype(np.int32)
    return count


def _bias2(d, slope=0.0):
    count = _multiplicity(d)
    return np.where(count > 0, np.log2(np.maximum(count, 1)) - slope * LOG2E * d, MASKED).astype(np.float32)


def _alibi_slopes(n_heads):
    return (2.0 ** (-8.0 * np.arange(1, n_heads + 1, dtype=np.float64) / n_heads)).astype(np.float32)


def _attn_prompt_kernel(slopes_ref, q_ref, k_ref, v_ref, logc_ref, o_ref, kb_scr, vb_scr, bias_scr,
                        *, n_off):
    h = pl.program_id(1)
    qi = pl.program_id(2)
    tq = bias_scr.shape[0]
    pad = (n_off - 1) * tq

    @pl.when(qi == 0)
    def _():
        kb_scr[0:pad, :] = jnp.zeros((pad, HEAD_DIM), BF16)
        vb_scr[0:pad, :] = jnp.zeros((pad, HEAD_DIM), BF16)
        kb_scr[pad:, :] = k_ref[...].astype(BF16)
        vb_scr[pad:, :] = v_ref[...].astype(BF16)
        slope = slopes_ref[h]
        row = lax.broadcasted_iota(jnp.int32, (tq, tq), 0)
        col = lax.broadcasted_iota(jnp.int32, (tq, tq), 1)
        base = (row - col).astype(F32)
        for j in range(n_off):
            m = n_off - 1 - j
            bias_scr[:, j * tq:(j + 1) * tq] = logc_ref[m] - (slope * LOG2E) * (base + float(m * tq))

    n_sub = q_ref.shape[0] // tq

    def blocks(masked):
        for sub in range(n_sub):
            _attn_prompt_block(qi * n_sub + sub, q_ref.at[sub * tq:(sub + 1) * tq],
                               o_ref.at[sub * tq:(sub + 1) * tq], kb_scr, vb_scr, bias_scr,
                               n_off=n_off, masked=masked)

    pl.when(qi * n_sub >= n_off - 1)(lambda: blocks(False))
    pl.when(qi * n_sub < n_off - 1)(lambda: blocks(True))


def _attn_prompt_block(qb, q_ref, o_ref, kb_scr, vb_scr, bias_scr, *, n_off, masked):
    tq = q_ref.shape[0]
    wlen = n_off * tq
    q = (q_ref[...] * (HEAD_DIM ** -0.5 * LOG2E)).astype(BF16)
    start = pl.multiple_of(qb * tq, tq)
    kw = kb_scr[pl.ds(start, wlen), :]
    vw = vb_scr[pl.ds(start, wlen), :]
    s = lax.dot_general(q, kw, (((1,), (1,)), ((), ())), preferred_element_type=F32) + bias_scr[...]
    if masked:
        lane = lax.broadcasted_iota(jnp.int32, (1, wlen), 1)
        s = s + jnp.where(lane < (n_off - 1 - qb) * tq, MASKED, 0.0)
    m_i = jnp.max(s, axis=-1, keepdims=True)
    p = jnp.exp2(s - m_i)
    l_i = jnp.sum(p, axis=-1, keepdims=True)
    acc = jnp.dot(p.astype(BF16), vw, preferred_element_type=F32)
    o_ref[...] = (acc / l_i).astype(o_ref.dtype)


def _attn_prompt(proj, n_batch, seq, n_heads):
    tq = ATTN_TQ
    rows = ATTN_QBLOCKS * tq
    nq = seq // rows
    n_off = MAX_WINDOW // tq + 1
    m = np.arange(n_off)[:, None, None]
    r = np.arange(tq)[None, :, None]
    c = np.arange(tq)[None, None, :]
    logc = _bias2(m * tq + r - c)
    kern = functools.partial(_attn_prompt_kernel, n_off=n_off)
    pad = (n_off - 1) * tq
    return pl.pallas_call(
        kern,
        grid=(n_batch, n_heads, nq),
        in_specs=[
            pl.BlockSpec(memory_space=pltpu.SMEM),
            pl.BlockSpec((rows, HEAD_DIM), lambda b, h, i: (b * nq + i, h)),
            pl.BlockSpec((seq, HEAD_DIM), lambda b, h, i: (b, n_heads + h)),
            pl.BlockSpec((seq, HEAD_DIM), lambda b, h, i: (b, 2 * n_heads + h)),
            pl.BlockSpec((n_off, tq, tq), lambda b, h, i: (0, 0, 0)),
        ],
        out_specs=pl.BlockSpec((rows, HEAD_DIM), lambda b, h, i: (b * nq + i, h)),
        out_shape=jax.ShapeDtypeStruct((n_batch * seq, n_heads * HEAD_DIM), BF16),
        scratch_shapes=[pltpu.VMEM((pad + seq, HEAD_DIM), BF16), pltpu.VMEM((pad + seq, HEAD_DIM), BF16),
                        pltpu.VMEM((tq, n_off * tq), F32)],
        compiler_params=_params("parallel", "parallel", "arbitrary"),
        name="attn_prompt",
    )(_alibi_slopes(n_heads), proj, proj, proj, logc)


def _attn_sample_kernel(q_ref, kn_ref, vn_ref, kf_ref, kr_ref, vf_ref, vr_ref,
                        bf_ref, br_ref, bn_ref, o_ref, *, heads):
    n_new = q_ref.shape[0]
    nt = (((1,), (1,)), ((), ()))

    def by_head(ref):
        return jnp.concatenate([ref[:, h * HEAD_DIM:(h + 1) * HEAD_DIM] for h in range(heads)], axis=0)

    q = (by_head(q_ref) * (HEAD_DIM ** -0.5 * LOG2E)).astype(BF16)
    pad = jnp.zeros((LANES - heads * n_new, HEAD_DIM), F32)
    kn = jnp.concatenate([by_head(kn_ref), pad], axis=0).astype(BF16)
    vn = jnp.concatenate([by_head(vn_ref), pad], axis=0).astype(BF16)
    kf = kf_ref[...].reshape(-1, HEAD_DIM).astype(BF16)
    kr = kr_ref[...].reshape(-1, HEAD_DIM).astype(BF16)
    s_f = lax.dot_general(q, kf, nt, preferred_element_type=F32) + bf_ref[...]
    s_r = lax.dot_general(q, kr, nt, preferred_element_type=F32) + br_ref[...]
    s_n = lax.dot_general(q, kn, nt, preferred_element_type=F32) + bn_ref[...]
    m = jnp.maximum(jnp.maximum(jnp.max(s_f, axis=-1, keepdims=True), jnp.max(s_r, axis=-1, keepdims=True)),
                    jnp.max(s_n, axis=-1, keepdims=True))
    p_f = jnp.exp2(s_f - m)
    p_r = jnp.exp2(s_r - m)
    p_n = jnp.exp2(s_n - m)
    l = (jnp.sum(p_f, axis=-1, keepdims=True) + jnp.sum(p_r, axis=-1, keepdims=True)
         + jnp.sum(p_n, axis=-1, keepdims=True))
    vf = vf_ref[...].reshape(-1, HEAD_DIM).astype(BF16)
    vr = vr_ref[...].reshape(-1, HEAD_DIM).astype(BF16)
    o = (jnp.dot(p_f.astype(BF16), vf, preferred_element_type=F32)
         + jnp.dot(p_r.astype(BF16), vr, preferred_element_type=F32)
         + jnp.dot(p_n.astype(BF16), vn, preferred_element_type=F32)) / l
    for h in range(heads):
        o_ref[:, h * HEAD_DIM:(h + 1) * HEAD_DIM] = o[h * n_new:(h + 1) * n_new]


def _attn_sample(proj, cache_k, cache_v, layer, n_seq, n_heads):
    depth, _, win, _, _ = cache_k.shape
    n_new = proj.shape[0] // n_seq
    hs = SUBLANES
    nhg = n_heads // hs
    (w_far, dil), (w_mid, _) = sorted(DILATED_GROUPS, reverse=True)[:2]
    assert n_new == SUBLANES and win % dil == 0 and (win - w_mid) % dil == 0 and win >= w_far >= w_mid
    n_chunk = win // dil
    far_chunks = (win - w_mid) // dil
    rec_chunks = n_chunk - far_chunks
    assert far_chunks % rec_chunks == 0
    kc = cache_k.reshape(depth * n_seq, n_chunk, dil, n_heads, HEAD_DIM)
    vc = cache_v.reshape(depth * n_seq, n_chunk, dil, n_heads, HEAD_DIM)

    slopes = _alibi_slopes(n_heads).astype(np.float64)
    same_head = np.arange(hs)[:, None, None, None, None] == np.arange(hs)[None, None, None, None, :]
    i = np.arange(n_new)[None, :, None, None, None]

    def table(chunks, rows):
        r = (chunks[:, None] * dil + rows[None, :])[None, None, :, :, None]
        d = np.broadcast_to(win + i - r, (hs, n_new, len(chunks), len(rows), hs))
        out = [np.where(same_head, _bias2(d, slopes[g * hs:(g + 1) * hs, None, None, None, None]), MASKED)
               for g in range(nhg)]
        return np.stack(out).reshape(nhg, hs * n_new, -1).astype(np.float32)

    chunks = np.arange(n_chunk)
    rows = np.arange(dil)
    bias_f = table(chunks[:far_chunks], rows[:n_new])
    bias_r = table(chunks[far_chunks:], rows)
    skipped_rows = (chunks[:far_chunks, None] * dil + rows[None, n_new:]).reshape(-1)
    assert not _multiplicity(win + np.arange(n_new)[:, None] - skipped_rows[None, :]).any()
    j = np.arange(LANES)
    hj, tj = j // n_new, j % n_new
    d_n = np.broadcast_to(np.arange(n_new)[None, :, None] - tj[None, None, :], (hs, n_new, LANES))
    own = hj[None, None, :] == np.arange(hs)[:, None, None]
    bias_n = np.stack([np.where(own, _bias2(d_n, slopes[g * hs:(g + 1) * hs, None, None]), MASKED)
                       for g in range(nhg)]).reshape(nhg, hs * n_new, LANES).astype(np.float32)

    wb = hs * HEAD_DIM
    q_blocks = n_heads * HEAD_DIM // wb
    base = layer * n_seq
    kern = functools.partial(_attn_sample_kernel, heads=hs)
    far_spec = pl.BlockSpec((None, far_chunks, n_new, hs, HEAD_DIM), lambda g, b: (base + b, 0, 0, g, 0))
    rec_spec = pl.BlockSpec((None, rec_chunks, dil, hs, HEAD_DIM),
                            lambda g, b: (base + b, far_chunks // rec_chunks, 0, g, 0))
    return pl.pallas_call(
        kern,
        grid=(nhg, n_seq),
        in_specs=[
            pl.BlockSpec((n_new, wb), lambda g, b: (b, g)),
            pl.BlockSpec((n_new, wb), lambda g, b: (b, q_blocks + g)),
            pl.BlockSpec((n_new, wb), lambda g, b: (b, 2 * q_blocks + g)),
            far_spec, rec_spec, far_spec, rec_spec,
            pl.BlockSpec((None,) + bias_f.shape[1:], lambda g, b: (g, 0, 0)),
            pl.BlockSpec((None,) + bias_r.shape[1:], lambda g, b: (g, 0, 0)),
            pl.BlockSpec((None,) + bias_n.shape[1:], lambda g, b: (g, 0, 0)),
        ],
        out_specs=pl.BlockSpec((n_new, wb), lambda g, b: (b, g)),
        out_shape=jax.ShapeDtypeStruct((n_seq * n_new, n_heads * HEAD_DIM), F32),
        compiler_params=_params("parallel", "parallel"),
        name="attn_sample",
    )(proj, proj, proj, kc, kc, vc, vc, bias_f, bias_r, bias_n)


def _s5_pack(a_re, a_im, log_dt, b_re, b_im, c_re, c_im):
    dt = jnp.exp(log_dt.astype(F32))[:, None]
    ar, ai = a_re.astype(F32), a_im.astype(F32)
    mag = jnp.exp(ar * dt)
    lb_re, lb_im = mag * jnp.cos(ai * dt), mag * jnp.sin(ai * dt)
    den = ar * ar + ai * ai
    ir, ii = ar / den, -ai / den
    cr = (lb_re - 1.0) * ir - lb_im * ii
    ci = (lb_re - 1.0) * ii + lb_im * ir
    bb_re = cr[..., None] * b_re - ci[..., None] * b_im
    bb_im = cr[..., None] * b_im + ci[..., None] * b_re
    n_blk = a_re.shape[0] // S5_GROUPS_PER_BLOCK
    gb, p, ch = S5_GROUPS_PER_BLOCK, SSM_STATE, SSM_GROUP_CH
    eye = jnp.eye(gb, dtype=F32)

    def pack_in(bb):
        return jnp.einsum('aGpc,GH->aGcHp', bb.reshape(n_blk, gb, p, ch), eye).reshape(n_blk, gb * ch, gb * p)

    def pack_out(cc):
        return jnp.einsum('aGcp,GH->aHpGc', cc.reshape(n_blk, gb, ch, p), eye).reshape(n_blk, gb * p, gb * ch)

    b_blk = jnp.concatenate([pack_in(bb_re), pack_in(bb_im)], axis=2).astype(BF16)
    c_blk = jnp.concatenate([pack_out(c_re.astype(F32)), -pack_out(c_im.astype(F32))], axis=1).astype(BF16)
    lbr = lb_re.reshape(n_blk, gb * p)
    lbi = lb_im.reshape(n_blk, gb * p)
    return b_blk, c_blk, lbr, lbi


def _s5_prompt_kernel(u_ref, b_ref, c_ref, lbr_ref, lbi_ref, d_ref, z_ref, sre_ref, sim_ref,
                      s_scr, xr_scr, xi_scr):
    i = pl.program_id(1)
    lc = u_ref.shape[0]
    n_blk = b_ref.shape[0]
    n_slab = s_scr.shape[0]
    half = n_slab // 2

    @pl.when(i == 0)
    def _():
        xr_scr[...] = jnp.zeros_like(xr_scr)
        xi_scr[...] = jnp.zeros_like(xi_scr)

    u = u_ref[...]
    ub = u.astype(BF16)
    for g in range(n_blk):
        bu = jnp.dot(ub[:, g * S5_BLOCK_CH:(g + 1) * S5_BLOCK_CH], b_ref[g], preferred_element_type=F32)
        for s in range(n_slab):
            s_scr[s, pl.ds(g, lc, stride=n_blk), :] = bu[:, s * LANES:(s + 1) * LANES]

    def step(t, carry):
        row = pl.multiple_of(t * n_blk, n_blk)
        new = []
        for s in range(half):
            xr, xi = carry[s], carry[half + s]
            lr = lbr_ref[:, s * LANES:(s + 1) * LANES]
            li = lbi_ref[:, s * LANES:(s + 1) * LANES]
            nr = lr * xr - li * xi + s_scr[s, pl.ds(row, n_blk), :]
            ni = lr * xi + li * xr + s_scr[half + s, pl.ds(row, n_blk), :]
            s_scr[s, pl.ds(row, n_blk), :] = nr
            s_scr[half + s, pl.ds(row, n_blk), :] = ni
            new.append((nr, ni))
        return tuple(n[0] for n in new) + tuple(n[1] for n in new)

    init = tuple(xr_scr[:, s * LANES:(s + 1) * LANES] for s in range(half)) + \
        tuple(xi_scr[:, s * LANES:(s + 1) * LANES] for s in range(half))
    fin = lax.fori_loop(0, lc, step, init, unroll=4)
    for s in range(half):
        xr_scr[:, s * LANES:(s + 1) * LANES] = fin[s]
        xi_scr[:, s * LANES:(s + 1) * LANES] = fin[half + s]
    sre_ref[...] = xr_scr[...]
    sim_ref[...] = xi_scr[...]

    for g in range(n_blk):
        xs = jnp.concatenate([s_scr[s, pl.ds(g, lc, stride=n_blk), :] for s in range(n_slab)], axis=1)
        y = jnp.dot(xs.astype(BF16), c_ref[g], preferred_element_type=F32)
        sl = slice(g * S5_BLOCK_CH, (g + 1) * S5_BLOCK_CH)
        y = y + d_ref[:, sl] * u[:, sl]
        z_ref[:, sl] = jax.nn.gelu(y)


def _s5_prompt(proj, n_batch, seq, u_col_block, b_blk, c_blk, lbr, lbi, d_skip, *, lc=S5_CHUNK):
    n_blk = b_blk.shape[0]
    assert n_blk == SUBLANES
    d_ssm = n_blk * S5_BLOCK_CH
    nst = S5_BLOCK_STATE
    nc = seq // lc
    n_slab = 2 * nst // LANES
    z, sre, sim = pl.pallas_call(
        _s5_prompt_kernel,
        grid=(n_batch, nc),
        in_specs=[
            pl.BlockSpec((lc, d_ssm), lambda b, i: (b * nc + i, u_col_block)),
            pl.BlockSpec(b_blk.shape, lambda b, i: (0, 0, 0), pipeline_mode=pl.Buffered(1)),
            pl.BlockSpec(c_blk.shape, lambda b, i: (0, 0, 0), pipeline_mode=pl.Buffered(1)),
            pl.BlockSpec(lbr.shape, lambda b, i: (0, 0)),
            pl.BlockSpec(lbi.shape, lambda b, i: (0, 0)),
            pl.BlockSpec((1, d_ssm), lambda b, i: (0, 0)),
        ],
        out_specs=[
            pl.BlockSpec((lc, d_ssm), lambda b, i: (b * nc + i, 0)),
            pl.BlockSpec((None, n_blk, nst), lambda b, i: (b, 0, 0)),
            pl.BlockSpec((None, n_blk, nst), lambda b, i: (b, 0, 0)),
        ],
        out_shape=[
            jax.ShapeDtypeStruct((n_batch * seq, d_ssm), F32),
            jax.ShapeDtypeStruct((n_batch, n_blk, nst), F32),
            jax.ShapeDtypeStruct((n_batch, n_blk, nst), F32),
        ],
        scratch_shapes=[pltpu.VMEM((n_slab, lc * n_blk, LANES), F32),
                        pltpu.VMEM((n_blk, nst), F32), pltpu.VMEM((n_blk, nst), F32)],
        compiler_params=_params("parallel", "arbitrary"),
        name="s5_prompt",
    )(proj, b_blk, c_blk, lbr, lbi, d_skip)
    return z, sre, sim


def _s5_sample_kernel(u_ref, b_ref, c_ref, lbr_ref, lbi_ref, d_ref, x0r_ref, x0i_ref,
                      z_ref, xr_ref, xi_ref):
    n_steps = u_ref.shape[0]
    lr = lbr_ref[...]
    li = lbi_ref[...]
    xr = x0r_ref[...]
    xi = x0i_ref[...]
    nst = xr.shape[1]
    for t in range(n_steps):
        u = u_ref[t]
        bu = jnp.dot(u.astype(BF16), b_ref[...], preferred_element_type=F32)
        xr, xi = lr * xr - li * xi + bu[:, :nst], lr * xi + li * xr + bu[:, nst:]
        xs = jnp.concatenate([xr, xi], axis=1).astype(BF16)
        y = jnp.dot(xs, c_ref[...], preferred_element_type=F32) + d_ref[...] * u
        z_ref[t] = jax.nn.gelu(y)
    xr_ref[...] = xr
    xi_ref[...] = xi


def _s5_sample(u_tm, b_blk, c_blk, lbr, lbi, d_skip, x0r, x0i):
    n_steps, n_seq, d_ssm = u_tm.shape
    n_blk = b_blk.shape[0]
    nst = S5_BLOCK_STATE
    bc = S5_BLOCK_CH
    return pl.pallas_call(
        _s5_sample_kernel,
        grid=(n_blk,),
        in_specs=[
            pl.BlockSpec((n_steps, n_seq, bc), lambda g: (0, 0, g)),
            pl.BlockSpec((None, bc, 2 * nst), lambda g: (g, 0, 0)),
            pl.BlockSpec((None, 2 * nst, bc), lambda g: (g, 0, 0)),
            pl.BlockSpec((None, 1, nst), lambda g: (g, 0, 0)),
            pl.BlockSpec((None, 1, nst), lambda g: (g, 0, 0)),
            pl.BlockSpec((1, bc), lambda g: (0, g)),
            pl.BlockSpec((n_seq, nst), lambda g: (0, g)),
            pl.BlockSpec((n_seq, nst), lambda g: (0, g)),
        ],
        out_specs=[
            pl.BlockSpec((n_steps, n_seq, bc), lambda g: (0, 0, g)),
            pl.BlockSpec((n_seq, nst), lambda g: (0, g)),
            pl.BlockSpec((n_seq, nst), lambda g: (0, g)),
        ],
        out_shape=[
            jax.ShapeDtypeStruct((n_steps, n_seq, d_ssm), F32),
            jax.ShapeDtypeStruct(x0r.shape, F32),
            jax.ShapeDtypeStruct(x0i.shape, F32),
        ],
        compiler_params=_params("parallel"),
        name="s5_sample",
    )(u_tm, b_blk, c_blk, lbr.reshape(n_blk, 1, nst), lbi.reshape(n_blk, 1, nst), d_skip, x0r, x0i)


def _glu_kernel(z_ref, w_ref, b_ref, o_ref):
    z = z_ref[...]
    gate = jnp.dot(z.astype(BF16), w_ref[...], preferred_element_type=F32) + b_ref[...]
    o_ref[...] = (z * jax.nn.sigmoid(gate)).astype(o_ref.dtype)


def _glu(z, w_b, b, *, tm):
    T, N = z.shape
    return pl.pallas_call(
        _glu_kernel,
        grid=(T // tm,),
        in_specs=[
            pl.BlockSpec((tm, N), lambda i: (i, 0)),
            pl.BlockSpec((N, N), lambda i: (0, 0), pipeline_mode=pl.Buffered(1)),
            pl.BlockSpec((1, N), lambda i: (0, 0)),
        ],
        out_specs=pl.BlockSpec((tm, N), lambda i: (i, 0)),
        out_shape=jax.ShapeDtypeStruct((T, N), BF16),
        compiler_params=_params("parallel"),
        name="glu",
    )(z, w_b, b)


def _out_proj_kernel(a1_ref, a2_ref, w1_ref, w2_ref, x_ref, o_ref):
    acc = jnp.dot(a1_ref[...], w1_ref[...], preferred_element_type=F32)
    acc = acc + jnp.dot(a2_ref[...], w2_ref[...], preferred_element_type=F32)
    o_ref[...] = x_ref[...] + acc


def _out_proj(o_att, o_ssm, w_b, x2d, *, tm, tn):
    T, D = x2d.shape
    k1, k2 = o_att.shape[1], o_ssm.shape[1]
    assert k1 == k2
    return pl.pallas_call(
        _out_proj_kernel,
        grid=(T // tm, D // tn),
        in_specs=[
            pl.BlockSpec((tm, k1), lambda i, j: (i, 0)),
            pl.BlockSpec((tm, k2), lambda i, j: (i, 0)),
            pl.BlockSpec((k1, tn), lambda i, j: (0, j)),
            pl.BlockSpec((k2, tn), lambda i, j: (1, j)),
            pl.BlockSpec((tm, tn), lambda i, j: (i, j)),
        ],
        out_specs=pl.BlockSpec((tm, tn), lambda i, j: (i, j)),
        out_shape=jax.ShapeDtypeStruct((T, D), F32),
        compiler_params=_params("parallel", "arbitrary"),
        name="out_proj",
    )(o_att, o_ssm, w_b, w_b, x2d)


def _ffn_kernel(*refs, seq_len, has_state, has_tail, n_main, n_chunk):
    n_grp = 7 if has_state else 5
    x_ref, halo_ref, g_ref = refs[:3]
    groups = [refs[3:3 + n_grp]]
    n_out = 2
    if has_tail:
        groups.append(refs[3 + n_grp:3 + 2 * n_grp])
        n_out = 3
    outs = refs[3 + len(groups) * n_grp:]
    y_ref, a_refs, h_scr = outs[0], outs[1:n_out], outs[n_out]
    i = pl.program_id(0)
    f = pl.program_id(1)
    tm = x_ref.shape[0]

    @pl.when(f == 0)
    def _():
        _rms_rows_to(h_scr, 0, halo_ref, g_ref)
        _rms_rows_to(h_scr, FFN_HALO, x_ref, g_ref)
        y_ref[...] = x_ref[...]

    def columns(group, a_ref):
        wg_ref, wu_ref, wd_ref, cw_ref, cb_ref = group[:5]
        tf = wg_ref.shape[1]
        a_ext = jnp.dot(h_scr[...], wg_ref[...], preferred_element_type=F32)
        a = a_ext[FFN_HALO:]
        pos = (i * tm + lax.broadcasted_iota(jnp.int32, (tm, tf), 0)) % seq_len
        prev1 = a_ext[FFN_HALO - 1:FFN_HALO - 1 + tm]
        prev2 = a_ext[FFN_HALO - 2:FFN_HALO - 2 + tm]
        if has_state:
            prev1 = jnp.where(pos >= 1, prev1, group[5][...])
            prev2 = jnp.where(pos >= 2, prev2, group[6][...])
        else:
            prev1 = jnp.where(pos >= 1, prev1, 0.0)
            prev2 = jnp.where(pos >= 2, prev2, 0.0)
        conv = cb_ref[...] + cw_ref[0:1, :] * prev2 + cw_ref[1:2, :] * prev1 + cw_ref[2:3, :] * a
        up = jnp.dot(h_scr[FFN_HALO:, :], wu_ref[...], preferred_element_type=F32)
        gated = (jax.nn.silu(conv) * up).astype(BF16)
        d_out = y_ref.shape[1]
        for n in range(d_out // n_chunk):
            sl = slice(n * n_chunk, (n + 1) * n_chunk)
            y_ref[:, sl] += jnp.dot(gated, wd_ref[:, sl], preferred_element_type=F32)
        a_ref[...] = a if has_state else a[tm - SUBLANES:]

    if has_tail:
        pl.when(f < n_main)(lambda: columns(groups[0], a_refs[0]))
        pl.when(f == n_main)(lambda: columns(groups[1], a_refs[1]))
    else:
        columns(groups[0], a_refs[0])


def _ffn(x1, g2, wg_b, wu_b, wd_b, conv_w, conv_b, seq_len, e1=None, e2=None, *, tm, tf):
    T, D = x1.shape
    F = wg_b.shape[1]
    has_state = e1 is not None
    assert seq_len % tm == 0 or tm % seq_len == 0
    n_main, tw = divmod(F, tf)
    has_tail = tw > 0
    assert n_main > 0 and (not has_tail or (n_main * tf) % tw == 0)
    blocks_per_halo = tm // FFN_HALO
    a_rows, a_blk = (T, tm) if has_state else (T // tm * SUBLANES, SUBLANES)

    def group(width, col, **mode):
        specs = [
            pl.BlockSpec((D, width), lambda i, f: (0, col(f)), **mode),
            pl.BlockSpec((D, width), lambda i, f: (0, col(f)), **mode),
            pl.BlockSpec((width, D), lambda i, f: (col(f), 0), **mode),
            pl.BlockSpec((CONV_W, width), lambda i, f: (0, col(f))),
            pl.BlockSpec((1, width), lambda i, f: (0, col(f))),
        ]
        ops = [wg_b, wu_b, wd_b, conv_w, conv_b]
        if has_state:
            specs += [pl.BlockSpec((tm, width), lambda i, f: (i, col(f)))] * 2
            ops += [e1, e2]
        return specs, ops

    in_specs = [
        pl.BlockSpec((tm, D), lambda i, f: (i, 0), pipeline_mode=pl.Buffered(1)),
        pl.BlockSpec((FFN_HALO, D), lambda i, f: (jnp.maximum(i * blocks_per_halo - 1, 0), 0)),
        pl.BlockSpec((1, D), lambda i, f: (0, 0)),
    ]
    args = [x1, x1, g2]
    main_col = lambda f: jnp.minimum(f, n_main - 1)
    specs, ops = group(tf, main_col)
    in_specs += specs
    args += ops
    out_specs = [pl.BlockSpec((tm, D), lambda i, f: (i, 0), pipeline_mode=pl.Buffered(1)),
                 pl.BlockSpec((a_blk, tf), lambda i, f: (i, main_col(f)))]
    out_shape = [jax.ShapeDtypeStruct((T, D), F32), jax.ShapeDtypeStruct((a_rows, n_main * tf), F32)]
    if has_tail:
        tail_col = n_main * tf // tw
        specs, ops = group(tw, lambda f: tail_col, pipeline_mode=pl.Buffered(1))
        in_specs += specs
        args += ops
        out_specs.append(pl.BlockSpec((a_blk, tw), lambda i, f: (i, 0)))
        out_shape.append(jax.ShapeDtypeStruct((a_rows, tw), F32))
    kern = functools.partial(_ffn_kernel, seq_len=seq_len, has_state=has_state, has_tail=has_tail,
                             n_main=n_main, n_chunk=min(D, 512))
    outs = pl.pallas_call(
        kern,
        grid=(T // tm, n_main + int(has_tail)),
        in_specs=in_specs,
        out_specs=out_specs,
        out_shape=out_shape,
        scratch_shapes=[pltpu.VMEM((FFN_HALO + tm, D), BF16)],
        compiler_params=_params("parallel", "arbitrary", vmem_limit_bytes=FFN_VMEM_LIMIT_BYTES),
        name="ffn",
    )(*args)
    a = jnp.concatenate(outs[1:], axis=1)
    return outs[0], a


def _layer(x3d, lw, *, layer=0, cache=None, ssm0=None, conv0=None):
    (norm1_g, w_in, q_norm_g, k_norm_g, a_re, a_im, log_dt, b_re, b_im, c_re, c_im, ssm_d,
     w_glu, b_glu, w_out, norm2_g, w_gate, w_up, conv_w, conv_b, w_down) = lw
    nb, L, D = x3d.shape
    T = nb * L
    x2d = x3d.reshape(T, D)
    d_ssm = ssm_d.shape[0]
    d_att = (w_in.shape[1] - d_ssm) // 3
    n_heads = d_att // HEAD_DIM
    n_groups = a_re.shape[0]
    tm = min(PROJ_TM, T)
    tm_ffn = min(FFN_TM, T)

    head_gain = jnp.concatenate([jnp.tile(q_norm_g.astype(F32), n_heads),
                                 jnp.tile(k_norm_g.astype(F32), n_heads)])[None]
    proj, k, v = _in_proj(x2d, norm1_g[None], w_in.astype(BF16), head_gain, n_heads,
                          tm=min(IN_PROJ_TM, T), tn=SUBLANES * HEAD_DIM)
    keep = min(MAX_WINDOW, L) if cache is None else L

    def head_major(kv):
        kv = kv.reshape(-1, nb, L, SUBLANES, HEAD_DIM)[:, :, L - keep:]
        return kv.transpose(1, 2, 0, 3, 4).reshape(nb, keep, n_heads, HEAD_DIM)

    k, v = head_major(k), head_major(v)

    b_blk, c_blk, lbr, lbi = _s5_pack(a_re, a_im, log_dt, b_re, b_im, c_re, c_im)
    d_skip = ssm_d.astype(F32)[None]
    if cache is None:
        o_att = _attn_prompt(proj, nb, L, n_heads)
        z, sre, sim = _s5_prompt(proj, nb, L, 3 * d_att // d_ssm, b_blk, c_blk, lbr, lbi, d_skip)
    else:
        o_att = _attn_sample(proj, cache[0], cache[1], layer, nb, n_heads).astype(BF16)
        u_tm = proj[:, 3 * d_att:].reshape(nb, L, d_ssm).transpose(1, 0, 2)
        z_tm, sre, sim = _s5_sample(u_tm, b_blk, c_blk, lbr, lbi, d_skip,
                                    ssm0[0].astype(F32).reshape(nb, n_groups * SSM_STATE),
                                    ssm0[1].astype(F32).reshape(nb, n_groups * SSM_STATE))
        z = z_tm.transpose(1, 0, 2).reshape(T, d_ssm)
    ssm_re = sre.reshape(nb, n_groups, SSM_STATE)
    ssm_im = sim.reshape(nb, n_groups, SSM_STATE)

    o_ssm = _glu(z, w_glu.astype(BF16), b_glu.astype(F32)[None], tm=tm)
    x1 = _out_proj(o_att, o_ssm, w_out.astype(BF16), x2d, tm=tm, tn=1024)

    ffn_w = (norm2_g[None], w_gate.astype(BF16), w_up.astype(BF16), w_down.astype(BF16),
             conv_w.astype(F32), conv_b.astype(F32)[None])
    F = w_gate.shape[1]
    if conv0 is None:
        y, a_tail = _ffn(x1, *ffn_w, L, tm=tm_ffn, tf=FFN_TF)
        tiles_per_seq = L // tm_ffn
        a_tail = a_tail.reshape(nb, tiles_per_seq, SUBLANES, F)
        conv_state = a_tail[:, -1, SUBLANES - (CONV_W - 1):]
    else:
        c0 = conv0.astype(F32)
        zeros = jnp.zeros((nb, L - 1, F), F32)
        e1 = jnp.concatenate([c0[:, 1:2], zeros], axis=1).reshape(T, F)
        e2 = jnp.concatenate([c0[:, 0:1], c0[:, 1:2], zeros[:, 1:]], axis=1).reshape(T, F)
        y, a_full = _ffn(x1, *ffn_w, L, e1, e2, tm=tm_ffn, tf=FFN_STATE_TF)
        conv_state = a_full.reshape(nb, L, F)[:, L - (CONV_W - 1):]
    return y.reshape(nb, L, D), k, v, ssm_re, ssm_im, conv_state


def kernel(x_prompt, x_sample, cache_k, cache_v, state_ssm_re, state_ssm_im, state_ffn_conv,
           norm1_g, w_in, q_norm_g, k_norm_g,
           ssm_a_re, ssm_a_im, ssm_log_dt, ssm_b_re, ssm_b_im, ssm_c_re, ssm_c_im,
           ssm_d, w_glu, b_glu, w_out, norm2_g,
           w_ffn_gate, w_ffn_up, ffn_conv_w, ffn_conv_b, w_ffn_down):
    depth = w_in.shape[0]
    outs_p = [[] for _ in range(5)]
    outs_s = [[] for _ in range(5)]
    yp, ys = x_prompt, x_sample
    for l in range(depth):
        lw = (norm1_g[l], w_in[l], q_norm_g[l], k_norm_g[l],
              ssm_a_re[l], ssm_a_im[l], ssm_log_dt[l], ssm_b_re[l], ssm_b_im[l], ssm_c_re[l], ssm_c_im[l],
              ssm_d[l], w_glu[l], b_glu[l], w_out[l], norm2_g[l],
              w_ffn_gate[l], w_ffn_up[l], ffn_conv_w[l], ffn_conv_b[l], w_ffn_down[l])
        yp, k_p, v_p, re_p, im_p, conv_p = _layer(yp, lw)
        ys, k_s, v_s, re_s, im_s, conv_s = _layer(
            ys, lw, layer=l, cache=(cache_k, cache_v),
            ssm0=(state_ssm_re[l], state_ssm_im[l]), conv0=state_ffn_conv[l])
        for lst, val in zip(outs_p, (k_p, v_p, re_p, im_p, conv_p)):
            lst.append(val)
        for lst, val in zip(outs_s, (k_s, v_s, re_s, im_s, conv_s)):
            lst.append(val)
    return (yp, ys, *(jnp.stack(o) for o in outs_p), *(jnp.stack(o) for o in outs_s))
```
